```python
import math
import jax
import jax.numpy as jnp
from jax import lax
import numpy as np

D_MODEL = 1024
BATCH = 4
SEQ = 8192
DEPTH = 2

MOBA_HEADS = 8
MOBA_HEAD_DIM = 64
MOBA_WIDTH = MOBA_HEADS * MOBA_HEAD_DIM
MOBA_BLOCK = 256
MOBA_TOPK = 3
MOBA_QCHUNK = 128
HGRN_HEADS = 4
HGRN_KDIM = 128
HGRN_VDIM = 128
HGRN_WIDTH = HGRN_HEADS * HGRN_VDIM
HGRN_CHUNK = 64
LOG_DECAY_MASK = -1e4
EXP_CLIP = 30.0
REL_BUCKETS = 32
REL_MAX_DIST = 2048
PLE_DIM = 256
RMS_EPS = 1e-6
NEG_INF = -1e30
IN_COLS = (MOBA_WIDTH,) * 4 + (HGRN_HEADS * HGRN_KDIM,) * 2 + (HGRN_WIDTH,) * 2 + (D_MODEL,) * 2
IN_WIDTH = sum(IN_COLS)

kernel_name = 'hybrid_moba_hgrn2_gated_merge'


def rmsnorm(x, gain):
    xf = x.astype(jnp.float32)
    return xf * lax.rsqrt(jnp.mean(xf * xf, axis=-1, keepdims=True) + RMS_EPS) * gain.astype(jnp.float32)


def t5_bucket(rel):
    n = jnp.maximum(rel, 0)
    max_exact = REL_BUCKETS // 2
    nf = jnp.maximum(n, max_exact).astype(jnp.float32)
    large = max_exact + (jnp.log(nf / max_exact) / math.log(REL_MAX_DIST / max_exact)
                         * (REL_BUCKETS - max_exact)).astype(jnp.int32)
    large = jnp.minimum(large, REL_BUCKETS - 1)
    return jnp.where(n < max_exact, n, large)


def moba_attention(q, k, v, rel_bias):
    b, h, s, dh = q.shape
    nb = -(-s // MOBA_BLOCK)
    s_pad = nb * MOBA_BLOCK
    pad = ((0, 0), (0, 0), (0, s_pad - s), (0, 0))
    q, k, v = jnp.pad(q, pad), jnp.pad(k, pad), jnp.pad(v, pad)
    k_blk = k.reshape(b, h, nb, MOBA_BLOCK, dh)
    v_blk = v.reshape(b, h, nb, MOBA_BLOCK, dh)
    k_mean = jnp.mean(k_blk, axis=3)
    q_blk = jnp.arange(s_pad) // MOBA_BLOCK
    gate = jnp.einsum('bhsd,bhnd->bhsn', q, k_mean)
    fully_past = jnp.arange(nb)[None, :] < q_blk[:, None]
    gate = jnp.where(fully_past, gate, NEG_INF)
    topk = min(MOBA_TOPK, nb)
    _, sel = lax.top_k(gate, topk)

    bias_table = rel_bias.astype(jnp.float32).T
    b_ix = jnp.arange(b)[:, None, None]
    h_ix = jnp.arange(h)[None, :, None]
    h4 = jnp.arange(h)[None, :, None, None]
    offs = jnp.arange(MOBA_BLOCK)
    scale = dh ** -0.5
    nqc = s_pad // MOBA_QCHUNK
    q_c = jnp.moveaxis(q.reshape(b, h, nqc, MOBA_QCHUNK, dh), 2, 0)
    sel_c = jnp.moveaxis(sel.reshape(b, h, nqc, MOBA_QCHUNK, topk), 2, 0)

    def chunk(args):
        qc, sc, c = args
        qpos = c * MOBA_QCHUNK + jnp.arange(MOBA_QCHUNK)
        blk = (c * MOBA_QCHUNK) // MOBA_BLOCK
        logits = []
        for j in range(topk):
            idx = sc[..., j]
            kg = k_blk[b_ix, h_ix, idx]
            rel = qpos[:, None] - (idx[..., None] * MOBA_BLOCK + offs)
            lg = jnp.einsum('bhqd,bhqkd->bhqk', qc, kg) * scale + bias_table[h4, t5_bucket(rel)]
            logits.append(jnp.where(j < blk, lg, NEG_INF))
        k_own = lax.dynamic_index_in_dim(k_blk, blk, axis=2, keepdims=False)
        v_own = lax.dynamic_index_in_dim(v_blk, blk, axis=2, keepdims=False)
        rel_own = qpos[:, None] - (blk * MOBA_BLOCK + offs)[None, :]
        lg_own = jnp.einsum('bhqd,bhkd->bhqk', qc, k_own) * scale + bias_table[:, t5_bucket(rel_own)]
        logits.append(jnp.where(rel_own >= 0, lg_own, NEG_INF))
        probs = jax.nn.softmax(jnp.concatenate(logits, axis=-1), axis=-1)
        out = jnp.einsum('bhqk,bhkd->bhqd', probs[..., topk * MOBA_BLOCK:], v_own)
        for j in range(topk):
            vg = v_blk[b_ix, h_ix, sc[..., j]]
            out = out + jnp.einsum('bhqk,bhqkd->bhqd', probs[..., j * MOBA_BLOCK:(j + 1) * MOBA_BLOCK], vg)
        return out

    out = lax.map(chunk, (q_c, sel_c, jnp.arange(nqc)))
    return jnp.moveaxis(out, 0, 2).reshape(b, h, s_pad, dh)[:, :, :s]


def hgrn2_chunkwise(q, k, v, g):
    b, s, h, dk = q.shape
    dv = v.shape[-1]
    nc = s // HGRN_CHUNK

    def chunks(t):
        return t.reshape(b, nc, HGRN_CHUNK, h, t.shape[-1]).transpose(1, 0, 3, 2, 4)

    causal = jnp.tril(jnp.ones((HGRN_CHUNK, HGRN_CHUNK), dtype=bool))

    def step(state, inp):
        qc, kc, vc, gc = inp
        cum = jnp.cumsum(gc, axis=2)
        o_inter = jnp.einsum('bhtk,bhkv->bhtv', qc * jnp.exp(cum), state)
        diff = jnp.where(causal[:, :, None], cum[:, :, :, None, :] - cum[:, :, None, :, :], LOG_DECAY_MASK)
        scores = jnp.einsum('bhtk,bhsk,bhtsk->bhts', qc, kc, jnp.exp(diff))
        o_intra = jnp.einsum('bhts,bhsv->bhtv', scores, vc)
        last = cum[:, :, -1:, :]
        state = state * jnp.exp(last[:, :, 0, :, None]) + jnp.einsum('bhsk,bhsv->bhkv', kc * jnp.exp(last - cum), vc)
        return state, o_inter + o_intra

    state0 = jnp.zeros((b, h, dk, dv), jnp.float32)
    _, out = lax.scan(step, state0, (chunks(q), chunks(k), chunks(v), chunks(g)))
    return out.transpose(1, 0, 3, 2, 4).reshape(b, s, h, dv)


def setup_inputs(seed: int = 0) -> dict:
    key = jax.random.key(seed)
    ks = jax.random.split(key, 14)
    f32 = jnp.float32
    nrm = lambda k, shp: jax.random.normal(k, shp, f32)
    return {
        'x': nrm(ks[0], (BATCH, SEQ, D_MODEL)),
        'p': nrm(ks[1], (DEPTH, BATCH, SEQ, PLE_DIM)),
        'norm_gain': 1.0 + 0.05 * nrm(ks[2], (DEPTH, D_MODEL)),
        'w_in': nrm(ks[3], (DEPTH, D_MODEL, IN_WIDTH)) * D_MODEL ** -0.5,
        'q_norm_gain': 1.0 + 0.05 * nrm(ks[4], (DEPTH, MOBA_HEAD_DIM)),
        'k_norm_gain': 1.0 + 0.05 * nrm(ks[5], (DEPTH, MOBA_HEAD_DIM)),
        'rel_bias': 0.5 * nrm(ks[6], (REL_BUCKETS, MOBA_HEADS)),
        'hgrn_lb_logits': 0.1 * nrm(ks[7], (DEPTH, HGRN_HEADS * HGRN_KDIM)),
        'hgrn_out_gain': 1.0 + 0.05 * nrm(ks[8], (DEPTH, HGRN_WIDTH)),
        'w_up_a': nrm(ks[9], (DEPTH, MOBA_WIDTH, D_MODEL)) * MOBA_WIDTH ** -0.5,
        'w_up_b': nrm(ks[10], (DEPTH, HGRN_WIDTH, D_MODEL)) * HGRN_WIDTH ** -0.5,
        'w_out': nrm(ks[11], (DEPTH, D_MODEL, D_MODEL)) * D_MODEL ** -0.5,
        'w_ple': nrm(ks[12], (DEPTH, PLE_DIM, D_MODEL)) * PLE_DIM ** -0.5,
        'w_ple_gate': nrm(ks[13], (DEPTH, D_MODEL, D_MODEL)) * D_MODEL ** -0.5,
    }


def reference(x, p, norm_gain, w_in, q_norm_gain, k_norm_gain, rel_bias, hgrn_lb_logits,
              hgrn_out_gain, w_up_a, w_up_b, w_out, w_ple, w_ple_gate):
    f32 = jnp.float32
    dt = x.dtype
    b, s, _ = x.shape
    split_pts = []
    acc = 0
    for w in IN_COLS[:-1]:
        acc += w
        split_pts.append(acc)
    lb_sm = jax.nn.softmax(hgrn_lb_logits.astype(f32), axis=0)
    lower_bounds = jnp.cumsum(lb_sm, axis=0) - lb_sm[0:1]

    h = x
    for i in range(DEPTH):
        xn = rmsnorm(h, norm_gain[i]).astype(dt)
        proj = xn @ w_in[i]
        aq, ak, av, ag, bq, bf, bi, bg, gate_a, gate_b = jnp.split(proj, split_pts, axis=-1)

        def heads(t):
            return t.reshape(b, s, MOBA_HEADS, MOBA_HEAD_DIM).transpose(0, 2, 1, 3)
        qa = rmsnorm(heads(aq), q_norm_gain[i])
        ka = rmsnorm(heads(ak), k_norm_gain[i])
        va = heads(av).astype(f32)
        ya = moba_attention(qa, ka, va, rel_bias).transpose(0, 2, 1, 3).reshape(b, s, MOBA_WIDTH)
        ya = (ya * jax.nn.silu(ag.astype(f32))).astype(dt) @ w_up_a[i]

        lb = lower_bounds[i].reshape(HGRN_HEADS, HGRN_KDIM)
        fpre = bf.astype(f32).reshape(b, s, HGRN_HEADS, HGRN_KDIM)
        log_f = jax.nn.log_sigmoid(fpre) + jnp.log1p(lb * jnp.exp(jnp.minimum(-fpre, EXP_CLIP)))
        kb = (1.0 - lb) * jax.nn.sigmoid(-fpre)
        qb = jax.nn.silu(bq.astype(f32)).reshape(b, s, HGRN_HEADS, HGRN_KDIM)
        vb = bi.astype(f32).reshape(b, s, HGRN_HEADS, HGRN_VDIM)
        yb = hgrn2_chunkwise(qb, kb, vb, log_f)
        yb = rmsnorm(yb, hgrn_out_gain[i].reshape(HGRN_HEADS, HGRN_VDIM)).reshape(b, s, HGRN_WIDTH)
        yb = (yb * jax.nn.silu(bg.astype(f32))).astype(dt) @ w_up_b[i]

        merged = jax.nn.sigmoid(gate_a.astype(f32)) * ya.astype(f32) + jax.nn.sigmoid(gate_b.astype(f32)) * yb.astype(f32)
        h = h + merged.astype(dt) @ w_out[i]

        ple = (p[i] @ w_ple[i]).astype(f32) * jax.nn.sigmoid((h @ w_ple_gate[i]).astype(f32))
        h = h + ple.astype(dt)
    return h
```

```python
import functools
import math

import jax
import jax.numpy as jnp
from jax import lax
from jax.experimental import pallas as pl
from jax.experimental.pallas import tpu as pltpu

F32 = jnp.float32
BF16 = jnp.bfloat16

MOBA_HEADS = 8
MOBA_HEAD_DIM = 64
MOBA_WIDTH = MOBA_HEADS * MOBA_HEAD_DIM
MOBA_BLOCK = 256
MOBA_TOPK = 3
HGRN_HEADS = 4
HGRN_DIM = 128
HGRN_WIDTH = HGRN_HEADS * HGRN_DIM
EXP_CLIP = 30.0
REL_BUCKETS = 32
REL_MAX_DIST = 2048
RMS_EPS = 1e-6
MASK_VALUE = -1e30

LANES = 128
SUBLANES = 8
VMEM_LIMIT_BYTES = 56 * 1024 * 1024

HEADS_PER_STEP = LANES // MOBA_HEAD_DIM
HEAD_PAIRS = MOBA_HEADS // HEADS_PER_STEP

_MAX_EXACT = REL_BUCKETS // 2
_LAST_BUCKET_DIST = math.ceil(_MAX_EXACT * (REL_MAX_DIST / _MAX_EXACT) ** ((REL_BUCKETS - 1 - _MAX_EXACT) / (REL_BUCKETS - _MAX_EXACT)))
NEAR_BLOCKS = -(-(_LAST_BUCKET_DIST + MOBA_BLOCK - 1) // MOBA_BLOCK)

HGRN_CHUNK = 64
HGRN_LEVELS = (32, 16, 8)
HGRN_DIAG = 8


def _sigmoid(x):
    return 1.0 / (1.0 + jnp.exp(-x))


def _silu(x):
    return x * _sigmoid(x)


def _dot(a, b):
    return jnp.dot(a, b, preferred_element_type=F32)


def _dot_nt(a, b):
    return lax.dot_general(a, b, (((1,), (1,)), ((), ())), preferred_element_type=F32)


def _dot_tn(a, b):
    return lax.dot_general(a, b, (((0,), (0,)), ((), ())), preferred_element_type=F32)


def _bias_tiles_kernel(tab_ref, out_ref):
    h = pl.program_id(0)
    d = pl.program_id(1)
    key = lax.broadcasted_iota(jnp.int32, (MOBA_BLOCK, MOBA_BLOCK), 0)
    qry = lax.broadcasted_iota(jnp.int32, (MOBA_BLOCK, MOBA_BLOCK), 1)
    rel = d * MOBA_BLOCK + qry - key
    n = jnp.maximum(rel, 0)
    nf = jnp.maximum(n, _MAX_EXACT).astype(F32)
    large = _MAX_EXACT + (jnp.log(nf / _MAX_EXACT) / math.log(REL_MAX_DIST / _MAX_EXACT)
                          * (REL_BUCKETS - _MAX_EXACT)).astype(jnp.int32)
    large = jnp.minimum(large, REL_BUCKETS - 1)
    bucket = jnp.where(n < _MAX_EXACT, n, large)
    val = jnp.zeros((MOBA_BLOCK, MOBA_BLOCK), F32)
    for b in range(REL_BUCKETS):
        val = jnp.where(bucket == b, tab_ref[b, h], val)
    out_ref[0, 0] = jnp.where(rel >= 0, val, MASK_VALUE)


def _bias_tiles(rel_bias):
    return pl.pallas_call(
        _bias_tiles_kernel,
        grid=(MOBA_HEADS, NEAR_BLOCKS),
        in_specs=[pl.BlockSpec(memory_space=pltpu.SMEM)],
        out_specs=pl.BlockSpec((1, 1, MOBA_BLOCK, MOBA_BLOCK), lambda h, d: (h, d, 0, 0)),
        out_shape=jax.ShapeDtypeStruct((MOBA_HEADS, NEAR_BLOCKS, MOBA_BLOCK, MOBA_BLOCK), F32),
        name="t5_bias_tiles",
    )(rel_bias.astype(F32))


def _in_proj_kernel(layer, x_ref, gain_ref, w_ref, qg_ref, kg_ref, lbl_ref, hmean_ref,
                    q_ref, k_ref, vt_ref, kmean_ref, ag_ref, hq_ref, hk_ref, hv_ref, hg_ref,
                    bg_ref, ga_ref, gb_ref):
    x = x_ref[0]
    xn = x * lax.rsqrt(jnp.mean(x * x, axis=-1, keepdims=True) + RMS_EPS) * gain_ref[...]
    xn = xn.astype(BF16)

    def seg(start, width):
        return _dot(xn, w_ref[0, :, start:start + width])

    def head_rms(t, gain):
        ms = _dot((t * t).astype(BF16), hmean_ref[...])
        return t * lax.rsqrt(ms + RMS_EPS) * gain

    w = MOBA_WIDTH
    q_ref[0] = head_rms(seg(0, w), qg_ref[...])
    kn = head_rms(seg(w, w), kg_ref[...])
    k_ref[0] = kn.astype(BF16)
    kmean_ref[0, 0] = jnp.mean(kn, axis=0, keepdims=True)
    vt_ref[0] = seg(2 * w, w).T.astype(BF16)
    ag_ref[0] = _silu(seg(3 * w, w))

    base = 4 * w
    hw = HGRN_WIDTH
    hq_ref[0] = _silu(seg(base, hw))
    lbl = lbl_ref[...]
    e = jnp.exp(lbl - jnp.max(lbl, axis=0, keepdims=True))
    sm = e / jnp.sum(e, axis=0, keepdims=True)
    lb = jnp.zeros((1, hw), F32)
    for j in range(1, layer + 1):
        lb = lb + sm[j:j + 1, :]
    z = seg(base + hw, hw)
    log_sig = jnp.minimum(z, 0.0) - jnp.log1p(jnp.exp(-jnp.abs(z)))
    hg_ref[0] = log_sig + jnp.log1p(lb * jnp.exp(jnp.minimum(-z, EXP_CLIP)))
    hk_ref[0] = (1.0 - lb) * _sigmoid(-z)
    hv_ref[0] = seg(base + 2 * hw, hw)
    bg_ref[0] = _silu(seg(base + 3 * hw, hw))

    base = base + 4 * hw
    d = x.shape[-1]
    ga_ref[0] = _sigmoid(seg(base, d))
    gb_ref[0] = _sigmoid(seg(base + d, d))


def _in_proj(layer, h, norm_gain, w_in, q_gain, k_gain, lb_logits, hmean):
    b, s, d = h.shape
    depth = w_in.shape[0]
    tm = MOBA_BLOCK
    nt = s // tm
    in_width = w_in.shape[-1]
    tok = lambda width: pl.BlockSpec((1, tm, width), lambda bi, ti: (bi, ti, 0))
    const2 = lambda shape: pl.BlockSpec(shape, lambda bi, ti: (0, 0))
    f32_tok = lambda width: jax.ShapeDtypeStruct((b, s, width), F32)
    outs = (
        (tok(MOBA_WIDTH), f32_tok(MOBA_WIDTH)),
        (tok(MOBA_WIDTH), jax.ShapeDtypeStruct((b, s, MOBA_WIDTH), BF16)),
        (pl.BlockSpec((1, MOBA_WIDTH, tm), lambda bi, ti: (bi, 0, ti)),
         jax.ShapeDtypeStruct((b, MOBA_WIDTH, s), BF16)),
        (pl.BlockSpec((1, 1, 1, MOBA_WIDTH), lambda bi, ti: (bi, ti, 0, 0)),
         jax.ShapeDtypeStruct((b, nt, 1, MOBA_WIDTH), F32)),
        (tok(MOBA_WIDTH), f32_tok(MOBA_WIDTH)),
        (tok(HGRN_WIDTH), f32_tok(HGRN_WIDTH)),
        (tok(HGRN_WIDTH), f32_tok(HGRN_WIDTH)),
        (tok(HGRN_WIDTH), f32_tok(HGRN_WIDTH)),
        (tok(HGRN_WIDTH), f32_tok(HGRN_WIDTH)),
        (tok(HGRN_WIDTH), f32_tok(HGRN_WIDTH)),
        (tok(d), f32_tok(d)),
        (tok(d), f32_tok(d)),
    )
    return pl.pallas_call(
        functools.partial(_in_proj_kernel, layer),
        grid=(b, nt),
        in_specs=[
            tok(d),
            const2((1, d)),
            pl.BlockSpec((1, d, in_width), lambda bi, ti: (layer, 0, 0), pipeline_mode=pl.Buffered(1)),
            const2((1, MOBA_WIDTH)),
            const2((1, MOBA_WIDTH)),
            const2((depth, HGRN_WIDTH)),
            const2((MOBA_WIDTH, MOBA_WIDTH)),
        ],
        out_specs=[o[0] for o in outs],
        out_shape=[o[1] for o in outs],
        compiler_params=pltpu.CompilerParams(
            dimension_semantics=("arbitrary", "arbitrary"), vmem_limit_bytes=VMEM_LIMIT_BYTES),
        name=f"in_proj_l{layer}",
    )(h, norm_gain, w_in, q_gain, k_gain, lb_logits, hmean)


def _split_bf16(x):
    hi = x.astype(BF16)
    lo = (x - hi.astype(F32)).astype(BF16)
    return hi, lo


def _moba_kernel(tab_ref, q_ref, k_ref, vt_ref, kmean_ref, bias_ref, out_ref, mask_ref):
    pair = pl.program_id(1)
    blk = pl.program_id(2)
    nb = kmean_ref.shape[1]
    tq = MOBA_BLOCK
    scale = MOBA_HEAD_DIM ** -0.5

    qt = q_ref[0].T
    feat = lax.broadcasted_iota(jnp.int32, (LANES, tq), 0)
    kblk = lax.broadcasted_iota(jnp.int32, (nb, tq), 0)
    kblk_f = kblk.astype(F32)
    km_hi, km_lo = _split_bf16(kmean_ref[0])

    qts = []
    for hh in range(HEADS_PER_STEP):
        in_head = (feat >= hh * MOBA_HEAD_DIM) & (feat < (hh + 1) * MOBA_HEAD_DIM)
        qt_h = jnp.where(in_head, qt, 0.0)
        qts.append((qt_h * scale).astype(BF16))

        q_hi, q_lo = _split_bf16(qt_h)
        gate = _dot(km_hi, q_hi) + _dot(km_hi, q_lo) + _dot(km_lo, q_hi)
        past = kblk < blk
        gate = jnp.where(past, gate, -jnp.inf)
        sel = jnp.zeros((nb, tq), jnp.bool_)
        for _ in range(MOBA_TOPK):
            best = jnp.max(gate, axis=0, keepdims=True)
            first = jnp.min(jnp.where(gate == best, kblk_f, float(nb)), axis=0, keepdims=True)
            hit = kblk_f == first
            sel = sel | hit
            gate = jnp.where(hit, -jnp.inf, gate)
        sel = sel & past
        far_bias = jnp.where(blk - kblk >= NEAR_BLOCKS, tab_ref[REL_BUCKETS - 1, pair * HEADS_PER_STEP + hh], 0.0)
        mask_ref[hh] = jnp.where(sel, far_bias, MASK_VALUE)

    def scores(j, hh):
        k_j = k_ref[0, pl.ds(pl.multiple_of(j * MOBA_BLOCK, MOBA_BLOCK), MOBA_BLOCK), :]
        return _dot(k_j, qts[hh])

    def values(j, hh):
        return vt_ref[0, hh * MOBA_HEAD_DIM:(hh + 1) * MOBA_HEAD_DIM,
                      pl.ds(pl.multiple_of(j * MOBA_BLOCK, MOBA_BLOCK), MOBA_BLOCK)]

    state = []
    for hh in range(HEADS_PER_STEP):
        s = scores(blk, hh) + bias_ref[hh, 0]
        m = jnp.max(s, axis=0, keepdims=True)
        p = jnp.exp(s - m)
        l = jnp.sum(p, axis=0, keepdims=True)
        acc = _dot(values(blk, hh), p.astype(BF16))
        state += [m, l, acc]

    def step(j, state, near):
        new = []
        for hh in range(HEADS_PER_STEP):
            m, l, acc = state[3 * hh:3 * hh + 3]
            s = scores(j, hh) + mask_ref[hh, pl.ds(j, 1), :]
            if near:
                s = s + bias_ref[hh, blk - j]
            m_new = jnp.maximum(m, jnp.max(s, axis=0, keepdims=True))
            alpha = jnp.exp(m - m_new)
            p = jnp.exp(s - m_new)
            l = alpha * l + jnp.sum(p, axis=0, keepdims=True)
            acc = alpha * acc + _dot(values(j, hh), p.astype(BF16))
            new += [m_new, l, acc]
        return new

    n_far = jnp.maximum(blk - (NEAR_BLOCKS - 1), 0)
    state = lax.fori_loop(0, n_far, lambda j, st: step(j, st, False), state)
    state = lax.fori_loop(n_far, blk, lambda j, st: step(j, st, True), state)

    out_t = jnp.concatenate([state[3 * hh + 2] / state[3 * hh + 1] for hh in range(HEADS_PER_STEP)], axis=0)
    out_ref[0] = out_t.T


def _moba(rel_bias, q, k, vt, kmean, bias_tiles):
    b, s, _ = q.shape
    nb = s // MOBA_BLOCK
    return pl.pallas_call(
        _moba_kernel,
        grid=(b, HEAD_PAIRS, nb),
        in_specs=[
            pl.BlockSpec(memory_space=pltpu.SMEM),
            pl.BlockSpec((1, MOBA_BLOCK, LANES), lambda bi, pi, ti: (bi, ti, pi)),
            pl.BlockSpec((1, s, LANES), lambda bi, pi, ti: (bi, 0, pi)),
            pl.BlockSpec((1, LANES, s), lambda bi, pi, ti: (bi, pi, 0)),
            pl.BlockSpec((1, nb, LANES), lambda bi, pi, ti: (bi, 0, pi)),
            pl.BlockSpec((HEADS_PER_STEP, NEAR_BLOCKS, MOBA_BLOCK, MOBA_BLOCK), lambda bi, pi, ti: (pi, 0, 0, 0)),
        ],
        out_specs=pl.BlockSpec((1, MOBA_BLOCK, LANES), lambda bi, pi, ti: (bi, ti, pi)),
        out_shape=jax.ShapeDtypeStruct((b, s, MOBA_WIDTH), F32),
        scratch_shapes=[pltpu.VMEM((HEADS_PER_STEP, nb, MOBA_BLOCK), F32)],
        compiler_params=pltpu.CompilerParams(
            dimension_semantics=("arbitrary", "arbitrary", "arbitrary"), vmem_limit_bytes=VMEM_LIMIT_BYTES),
        name="moba_attention",
    )(rel_bias.astype(F32), q, k, vt, kmean, bias_tiles)


def _chunk_cumsum(g):
    groups = HGRN_CHUNK // SUBLANES
    g3 = g.reshape(groups, SUBLANES, g.shape[-1])
    sub = lax.broadcasted_iota(jnp.int32, g3.shape, 1)
    shift = 1
    while shift < SUBLANES:
        g3 = g3 + jnp.where(sub >= shift, pltpu.roll(g3, shift, axis=1), 0.0)
        shift *= 2
    rows = []
    run = None
    for i in range(groups):
        cur = g3[i] if run is None else g3[i] + run
        rows.append(cur)
        run = cur[SUBLANES - 1:SUBLANES, :]
    return jnp.concatenate(rows, axis=0)


def _hgrn_chunk(q, k, v, g, state_t):
    c = HGRN_CHUNK
    cum = _chunk_cumsum(g)
    last = cum[c - 1:c, :]
    v16 = v.astype(BF16)

    o = _dot_nt((q * jnp.exp(cum)).astype(BF16), state_t.astype(BF16))
    k_end = (k * jnp.exp(last - cum)).astype(BF16)
    new_state = state_t * jnp.exp(last) + _dot_tn(v16, k_end)

    row = lax.broadcasted_iota(jnp.int32, (c, 1), 0)
    rt = lax.broadcasted_iota(jnp.int32, (c, c), 0)
    cs = lax.broadcasted_iota(jnp.int32, (c, c), 1)
    scores = jnp.zeros((c, c), F32)

    for half in HGRN_LEVELS:
        refs = []
        for lo in range(0, c, 2 * half):
            r = lo + half - 1
            refs.append(jnp.broadcast_to(cum[r:r + 1, :], (2 * half, cum.shape[-1])))
        ref = jnp.concatenate(refs, axis=0)
        is_q = (row & half) != 0
        x = (jnp.where(is_q, q, k) * jnp.exp(jnp.where(is_q, cum - ref, ref - cum))).astype(BF16)
        block_scores = _dot_nt(x, x)
        shift = int(math.log2(2 * half))
        valid = ((rt >> shift) == (cs >> shift)) & ((rt & half) != 0) & ((cs & half) == 0)
        scores = scores + jnp.where(valid, block_scores, 0.0)

    groups = c // HGRN_DIAG
    cum3 = cum.reshape(groups, HGRN_DIAG, cum.shape[-1])
    q3 = q.reshape(groups, HGRN_DIAG, q.shape[-1])
    k3 = k.reshape(groups, HGRN_DIAG, k.shape[-1])
    for s in range(HGRN_DIAG):
        decay = jnp.exp(jnp.minimum(cum3 - cum3[:, s:s + 1, :], 0.0))
        col = jnp.sum(q3 * decay * k3[:, s:s + 1, :], axis=-1, keepdims=True).reshape(c, 1)
        here = (cs == (rt & ~(HGRN_DIAG - 1)) + s) & ((rt & (HGRN_DIAG - 1)) >= s)
        scores = jnp.where(here, col, scores)

    o = o + _dot(scores.astype(BF16), v16)
    return o, new_state


def _hgrn_kernel(q_ref, k_ref, v_ref, g_ref, out_ref, state_ref):
    @pl.when(pl.program_id(1) == 0)
    def _():
        state_ref[...] = jnp.zeros_like(state_ref)

    n_chunks = q_ref.shape[1] // HGRN_CHUNK

    def chunk(ci, carry):
        rows = pl.ds(pl.multiple_of(ci * HGRN_CHUNK, HGRN_CHUNK), HGRN_CHUNK)
        for h in range(HGRN_HEADS):
            cols = slice(h * HGRN_DIM, (h + 1) * HGRN_DIM)
            o, new_state = _hgrn_chunk(q_ref[0, rows, cols], k_ref[0, rows, cols], v_ref[0, rows, cols],
                                       g_ref[0, rows, cols], state_ref[h])
            out_ref[0, rows, cols] = o
            state_ref[h] = new_state
        return carry

    lax.fori_loop(0, n_chunks, chunk, 0)


def _hgrn(hq, hk, hv, hg):
    b, s, w = hq.shape
    ts = min(s, 512)
    spec = pl.BlockSpec((1, ts, w), lambda bi, ti: (bi, ti, 0))
    return pl.pallas_call(
        _hgrn_kernel,
        grid=(b, s // ts),
        in_specs=[spec, spec, spec, spec],
        out_specs=spec,
        out_shape=jax.ShapeDtypeStruct((b, s, w), F32),
        scratch_shapes=[pltpu.VMEM((HGRN_HEADS, HGRN_DIM, HGRN_DIM), F32)],
        compiler_params=pltpu.CompilerParams(
            dimension_semantics=("arbitrary", "arbitrary"), vmem_limit_bytes=VMEM_LIMIT_BYTES),
        name="hgrn2_recurrence",
    )(hq, hk, hv, hg)


def _merge_kernel(h_ref, ya_ref, ag_ref, yb_ref, bg_ref, ga_ref, gb_ref, p_ref, og_ref,
                  wa_ref, wb_ref, wo_ref, wp_ref, wg_ref, out_ref):
    ya = _dot((ya_ref[0] * ag_ref[0]).astype(BF16), wa_ref[0])

    yb = yb_ref[0]
    normed = []
    for h in range(HGRN_HEADS):
        t = yb[:, h * HGRN_DIM:(h + 1) * HGRN_DIM]
        normed.append(t * lax.rsqrt(jnp.mean(t * t, axis=-1, keepdims=True) + RMS_EPS))
    yb = jnp.concatenate(normed, axis=-1) * og_ref[...]
    yb = _dot((yb * bg_ref[0]).astype(BF16), wb_ref[0])

    merged = ga_ref[0] * ya + gb_ref[0] * yb
    h = h_ref[0] + _dot(merged.astype(BF16), wo_ref[0])
    ple = _dot(p_ref[0, 0].astype(BF16), wp_ref[0]) * _sigmoid(_dot(h.astype(BF16), wg_ref[0]))
    out_ref[0] = h + ple


def _merge(layer, h, ya, ag, yb, bg, ga, gb, p, out_gain, w_up_a, w_up_b, w_out, w_ple, w_ple_gate):
    b, s, d = h.shape
    tm = min(s, 512)
    tok = lambda width: pl.BlockSpec((1, tm, width), lambda bi, ti: (bi, ti, 0))
    wspec = lambda w: pl.BlockSpec((1,) + w.shape[1:], lambda bi, ti: (layer, 0, 0), pipeline_mode=pl.Buffered(1))
    return pl.pallas_call(
        _merge_kernel,
        grid=(b, s // tm),
        in_specs=[
            tok(d), tok(MOBA_WIDTH), tok(MOBA_WIDTH), tok(HGRN_WIDTH), tok(HGRN_WIDTH), tok(d), tok(d),
            pl.BlockSpec((1, 1, tm, p.shape[-1]), lambda bi, ti: (layer, bi, ti, 0)),
            pl.BlockSpec((1, HGRN_WIDTH), lambda bi, ti: (0, 0)),
            wspec(w_up_a), wspec(w_up_b), wspec(w_out), wspec(w_ple), wspec(w_ple_gate),
        ],
        out_specs=tok(d),
        out_shape=jax.ShapeDtypeStruct((b, s, d), F32),
        compiler_params=pltpu.CompilerParams(
            dimension_semantics=("arbitrary", "arbitrary"), vmem_limit_bytes=VMEM_LIMIT_BYTES),
        name=f"merge_l{layer}",
    )(h, ya, ag, yb, bg, ga, gb, p, out_gain, w_up_a, w_up_b, w_out, w_ple, w_ple_gate)


def kernel(x, p, norm_gain, w_in, q_norm_gain, k_norm_gain, rel_bias, hgrn_lb_logits, hgrn_out_gain,
           w_up_a, w_up_b, w_out, w_ple, w_ple_gate):
    b, s, d = x.shape
    depth = w_in.shape[0]
    assert s % MOBA_BLOCK == 0 and s % HGRN_CHUNK == 0

    w_in, w_up_a, w_up_b, w_out, w_ple, w_ple_gate = (
        w.astype(BF16) for w in (w_in, w_up_a, w_up_b, w_out, w_ple, w_ple_gate))
    head_id = jnp.arange(MOBA_WIDTH) // MOBA_HEAD_DIM
    hmean = ((head_id[:, None] == head_id[None, :]).astype(F32) / MOBA_HEAD_DIM).astype(BF16)
    tile = lambda g: jnp.tile(g.astype(F32), MOBA_HEADS)[None, :]

    bias_tiles = _bias_tiles(rel_bias)
    h = x
    for i in range(depth):
        q, k, vt, kmean, ag, hq, hk, hv, hg, bg, ga, gb = _in_proj(
            i, h, norm_gain[i][None, :].astype(F32), w_in, tile(q_norm_gain[i]), tile(k_norm_gain[i]),
            hgrn_lb_logits.astype(F32), hmean)
        ya = _moba(rel_bias, q, k, vt, kmean.reshape(b, s // MOBA_BLOCK, MOBA_WIDTH), bias_tiles)
        yb = _hgrn(hq, hk, hv, hg)
        h = _merge(i, h, ya, ag, yb, bg, ga, gb, p, hgrn_out_gain[i][None, :].astype(F32),
                   w_up_a, w_up_b, w_out, w_ple, w_ple_gate)
    return h
```

```python
import functools
import math

import jax
import jax.numpy as jnp
from jax import lax
from jax.experimental import pallas as pl
from jax.experimental.pallas import tpu as pltpu

F32 = jnp.float32
BF16 = jnp.bfloat16

MOBA_HEADS = 8
MOBA_HEAD_DIM = 64
MOBA_WIDTH = MOBA_HEADS * MOBA_HEAD_DIM
MOBA_BLOCK = 256
MOBA_TOPK = 3
HGRN_HEADS = 4
HGRN_DIM = 128
HGRN_WIDTH = HGRN_HEADS * HGRN_DIM
EXP_CLIP = 30.0
REL_BUCKETS = 32
REL_MAX_DIST = 2048
RMS_EPS = 1e-6
MASK_VALUE = -1e30

LANES = 128
SUBLANES = 8
VMEM_LIMIT_BYTES = 56 * 1024 * 1024

LOG2E = math.log2(math.e)
HEADS_PER_PAIR = LANES // MOBA_HEAD_DIM
MOBA_PAIRS_PER_STEP = 2
MOBA_HEADS_PER_STEP = HEADS_PER_PAIR * MOBA_PAIRS_PER_STEP
MOBA_GROUP = 4

_MAX_EXACT = REL_BUCKETS // 2
_LAST_BUCKET_DIST = math.ceil(_MAX_EXACT * (REL_MAX_DIST / _MAX_EXACT) ** ((REL_BUCKETS - 1 - _MAX_EXACT) / (REL_BUCKETS - _MAX_EXACT)))
NEAR_BLOCKS = -(-(_LAST_BUCKET_DIST + MOBA_BLOCK - 1) // MOBA_BLOCK)

HGRN_CHUNK = 64
HGRN_LEVELS = (32, 16, 8)
HGRN_DIAG = 8


def _sigmoid(x):
    return 1.0 / (1.0 + jnp.exp(-x))


def _silu(x):
    return x * _sigmoid(x)


def _dot(a, b):
    return jnp.dot(a, b, preferred_element_type=F32)


def _dot_nt(a, b):
    return lax.dot_general(a, b, (((1,), (1,)), ((), ())), preferred_element_type=F32)


def _dot_tn(a, b):
    return lax.dot_general(a, b, (((0,), (0,)), ((), ())), preferred_element_type=F32)


def _bias_tiles_kernel(tab_ref, out_ref):
    h = pl.program_id(0)
    d = pl.program_id(1)
    key = lax.broadcasted_iota(jnp.int32, (MOBA_BLOCK, MOBA_BLOCK), 0)
    qry = lax.broadcasted_iota(jnp.int32, (MOBA_BLOCK, MOBA_BLOCK), 1)
    rel = d * MOBA_BLOCK + qry - key
    n = jnp.maximum(rel, 0)
    nf = jnp.maximum(n, _MAX_EXACT).astype(F32)
    large = _MAX_EXACT + (jnp.log(nf / _MAX_EXACT) / math.log(REL_MAX_DIST / _MAX_EXACT)
                          * (REL_BUCKETS - _MAX_EXACT)).astype(jnp.int32)
    large = jnp.minimum(large, REL_BUCKETS - 1)
    bucket = jnp.where(n < _MAX_EXACT, n, large)
    val = jnp.zeros((MOBA_BLOCK, MOBA_BLOCK), F32)
    for b in range(REL_BUCKETS):
        val = jnp.where(bucket == b, tab_ref[b, h], val)
    val = jnp.where(rel >= 0, val * LOG2E, MASK_VALUE)
    out_ref[0, 0] = jnp.where(d < NEAR_BLOCKS, val, 0.0)


def _bias_tiles(rel_bias):
    return pl.pallas_call(
        _bias_tiles_kernel,
        grid=(MOBA_HEADS, NEAR_BLOCKS + 1),
        in_specs=[pl.BlockSpec(memory_space=pltpu.SMEM)],
        out_specs=pl.BlockSpec((1, 1, MOBA_BLOCK, MOBA_BLOCK), lambda h, d: (h, d, 0, 0)),
        out_shape=jax.ShapeDtypeStruct((MOBA_HEADS, NEAR_BLOCKS + 1, MOBA_BLOCK, MOBA_BLOCK), F32),
        name="t5_bias_tiles",
    )(rel_bias.astype(F32))


def _in_proj_kernel(layer, x_ref, gain_ref, w_ref, qg_ref, kg_ref, lbl_ref, hmean_ref,
                    q_ref, k_ref, vt_ref, kmean_ref, ag_ref, hq_ref, hk_ref, hv_ref, hg_ref,
                    bg_ref, ga_ref, gb_ref):
    x = x_ref[0]
    xn = x * lax.rsqrt(jnp.mean(x * x, axis=-1, keepdims=True) + RMS_EPS) * gain_ref[...]
    xn = xn.astype(BF16)

    def seg(start, width):
        return _dot(xn, w_ref[0, :, start:start + width])

    def head_rms(t, gain):
        ms = _dot((t * t).astype(BF16), hmean_ref[...])
        return t * lax.rsqrt(ms + RMS_EPS) * gain

    w = MOBA_WIDTH
    q_ref[0] = head_rms(seg(0, w), qg_ref[...])
    kn = head_rms(seg(w, w), kg_ref[...])
    k_ref[0] = kn.astype(BF16)
    kmean_ref[0, 0] = jnp.mean(kn, axis=0, keepdims=True)
    vt_ref[0] = seg(2 * w, w).T.astype(BF16)
    ag_ref[0] = _silu(seg(3 * w, w))

    base = 4 * w
    hw = HGRN_WIDTH
    hq_ref[0] = _silu(seg(base, hw))
    lbl = lbl_ref[...]
    e = jnp.exp(lbl - jnp.max(lbl, axis=0, keepdims=True))
    sm = e / jnp.sum(e, axis=0, keepdims=True)
    lb = jnp.zeros((1, hw), F32)
    for j in range(1, layer + 1):
        lb = lb + sm[j:j + 1, :]
    z = seg(base + hw, hw)
    log_sig = jnp.minimum(z, 0.0) - jnp.log1p(jnp.exp(-jnp.abs(z)))
    hg_ref[0] = log_sig + jnp.log1p(lb * jnp.exp(jnp.minimum(-z, EXP_CLIP)))
    hk_ref[0] = (1.0 - lb) * _sigmoid(-z)
    hv_ref[0] = seg(base + 2 * hw, hw)
    bg_ref[0] = _silu(seg(base + 3 * hw, hw))

    base = base + 4 * hw
    d = x.shape[-1]
    ga_ref[0] = _sigmoid(seg(base, d))
    gb_ref[0] = _sigmoid(seg(base + d, d))


def _in_proj(layer, h, norm_gain, w_in, q_gain, k_gain, lb_logits, hmean):
    b, s, d = h.shape
    depth = w_in.shape[0]
    tm = MOBA_BLOCK
    nt = s // tm
    in_width = w_in.shape[-1]
    tok = lambda width: pl.BlockSpec((1, tm, width), lambda bi, ti: (bi, ti, 0))
    const2 = lambda shape: pl.BlockSpec(shape, lambda bi, ti: (0, 0))
    f32_tok = lambda width: jax.ShapeDtypeStruct((b, s, width), F32)
    outs = (
        (tok(MOBA_WIDTH), f32_tok(MOBA_WIDTH)),
        (tok(MOBA_WIDTH), jax.ShapeDtypeStruct((b, s, MOBA_WIDTH), BF16)),
        (pl.BlockSpec((1, MOBA_WIDTH, tm), lambda bi, ti: (bi, 0, ti)),
         jax.ShapeDtypeStruct((b, MOBA_WIDTH, s), BF16)),
        (pl.BlockSpec((1, 1, 1, MOBA_WIDTH), lambda bi, ti: (bi, ti, 0, 0)),
         jax.ShapeDtypeStruct((b, nt, 1, MOBA_WIDTH), F32)),
        (tok(MOBA_WIDTH), f32_tok(MOBA_WIDTH)),
        (tok(HGRN_WIDTH), f32_tok(HGRN_WIDTH)),
        (tok(HGRN_WIDTH), f32_tok(HGRN_WIDTH)),
        (tok(HGRN_WIDTH), f32_tok(HGRN_WIDTH)),
        (tok(HGRN_WIDTH), f32_tok(HGRN_WIDTH)),
        (tok(HGRN_WIDTH), f32_tok(HGRN_WIDTH)),
        (tok(d), f32_tok(d)),
        (tok(d), f32_tok(d)),
    )
    return pl.pallas_call(
        functools.partial(_in_proj_kernel, layer),
        grid=(b, nt),
        in_specs=[
            tok(d),
            const2((1, d)),
            pl.BlockSpec((1, d, in_width), lambda bi, ti: (layer, 0, 0), pipeline_mode=pl.Buffered(1)),
            const2((1, MOBA_WIDTH)),
            const2((1, MOBA_WIDTH)),
            const2((depth, HGRN_WIDTH)),
            const2((MOBA_WIDTH, MOBA_WIDTH)),
        ],
        out_specs=[o[0] for o in outs],
        out_shape=[o[1] for o in outs],
        compiler_params=pltpu.CompilerParams(
            dimension_semantics=("arbitrary", "arbitrary"), vmem_limit_bytes=VMEM_LIMIT_BYTES),
        name=f"in_proj_l{layer}",
    )(h, norm_gain, w_in, q_gain, k_gain, lb_logits, hmean)


def _split_bf16(x):
    hi = x.astype(BF16)
    lo = (x - hi.astype(F32)).astype(BF16)
    return hi, lo


def _moba_kernel(tab_ref, q_ref, k_ref, vt_ref, kmean_ref, bias_ref, out_ref, mask_ref):
    step_heads = pl.program_id(1) * MOBA_HEADS_PER_STEP
    blk = pl.program_id(2)
    nb = kmean_ref.shape[1]
    tq = MOBA_BLOCK
    group_rows = MOBA_GROUP * MOBA_BLOCK
    scale = MOBA_HEAD_DIM ** -0.5 * LOG2E

    feat = lax.broadcasted_iota(jnp.int32, (LANES, tq), 0)
    kblk = lax.broadcasted_iota(jnp.int32, (nb, tq), 0)
    kblk_f = kblk.astype(F32)
    past = kblk < blk

    qts = []
    for pp in range(MOBA_PAIRS_PER_STEP):
        lanes = slice(pp * LANES, (pp + 1) * LANES)
        qt = q_ref[0, :, lanes].T
        km_hi, km_lo = _split_bf16(kmean_ref[0, :, lanes])
        for hh in range(HEADS_PER_PAIR):
            h = pp * HEADS_PER_PAIR + hh
            in_head = (feat >= hh * MOBA_HEAD_DIM) & (feat < (hh + 1) * MOBA_HEAD_DIM)
            qt_h = jnp.where(in_head, qt, 0.0)
            qts.append((qt_h * scale).astype(BF16))

            q_hi, q_lo = _split_bf16(qt_h)
            gate = _dot(km_hi, q_hi) + _dot(km_hi, q_lo) + _dot(km_lo, q_hi)
            gate = jnp.where(past, gate, -jnp.inf)
            sel = jnp.zeros((nb, tq), jnp.bool_)
            for _ in range(MOBA_TOPK):
                best = jnp.max(gate, axis=0, keepdims=True)
                first = jnp.min(jnp.where(gate == best, kblk_f, float(nb)), axis=0, keepdims=True)
                hit = kblk_f == first
                sel = sel | hit
                gate = jnp.where(hit, -jnp.inf, gate)
            sel = (sel & past) | (kblk == blk)
            far_bias = jnp.where(blk - kblk >= NEAR_BLOCKS, tab_ref[REL_BUCKETS - 1, step_heads + h] * LOG2E, 0.0)
            mask_ref[h] = jnp.where(sel, far_bias, MASK_VALUE)

    def group(g, state, near):
        rows = pl.ds(pl.multiple_of(g * group_rows, group_rows), group_rows)
        new = []
        s_heads = [_dot(k_ref[0, rows, (h // HEADS_PER_PAIR) * LANES:(h // HEADS_PER_PAIR + 1) * LANES], qts[h])
                   for h in range(MOBA_HEADS_PER_STEP)]
        for h in range(MOBA_HEADS_PER_STEP):
            pp, hh = divmod(h, HEADS_PER_PAIR)
            m, l, acc = state[3 * h:3 * h + 3]
            s_all = s_heads[h]
            s_blocks = []
            for u in range(MOBA_GROUP):
                j = g * MOBA_GROUP + u
                s = s_all[u * MOBA_BLOCK:(u + 1) * MOBA_BLOCK] + mask_ref[h, pl.ds(j, 1), :]
                if near:
                    dist = blk - j
                    s = s + bias_ref[h, jnp.where((dist >= 0) & (dist < NEAR_BLOCKS), dist, NEAR_BLOCKS)]
                s_blocks.append(s)
            m_new = m
            for s in s_blocks:
                m_new = jnp.maximum(m_new, jnp.max(s, axis=0, keepdims=True))
            alpha = jnp.exp2(m - m_new)
            p_blocks = [jnp.exp2(s - m_new) for s in s_blocks]
            l = alpha * l
            for p in p_blocks:
                l = l + jnp.sum(p, axis=0, keepdims=True)
            p_all = jnp.concatenate([p.astype(BF16) for p in p_blocks], axis=0)
            vt = vt_ref[0, pp * LANES + hh * MOBA_HEAD_DIM:pp * LANES + (hh + 1) * MOBA_HEAD_DIM, rows]
            acc = alpha * acc + _dot(vt, p_all)
            new += [m_new, l, acc]
        return new

    state = []
    for _ in range(MOBA_HEADS_PER_STEP):
        state += [jnp.full((1, tq), MASK_VALUE, F32), jnp.zeros((1, tq), F32), jnp.zeros((MOBA_HEAD_DIM, tq), F32)]
    n_far = jnp.maximum(blk - (NEAR_BLOCKS - 1), 0) // MOBA_GROUP
    state = lax.fori_loop(0, n_far, lambda g, st: group(g, st, False), state)
    state = lax.fori_loop(n_far, blk // MOBA_GROUP + 1, lambda g, st: group(g, st, True), state)

    out_t = jnp.concatenate([state[3 * h + 2] / state[3 * h + 1] for h in range(MOBA_HEADS_PER_STEP)], axis=0)
    out_ref[0] = out_t.T


def _moba(rel_bias, q, k, vt, kmean, bias_tiles):
    b, s, _ = q.shape
    nb = s // MOBA_BLOCK
    assert nb % MOBA_GROUP == 0
    width = MOBA_PAIRS_PER_STEP * LANES
    return pl.pallas_call(
        _moba_kernel,
        grid=(b, MOBA_HEADS // MOBA_HEADS_PER_STEP, nb),
        in_specs=[
            pl.BlockSpec(memory_space=pltpu.SMEM),
            pl.BlockSpec((1, MOBA_BLOCK, width), lambda bi, pi, ti: (bi, ti, pi)),
            pl.BlockSpec((1, s, width), lambda bi, pi, ti: (bi, 0, pi)),
            pl.BlockSpec((1, width, s), lambda bi, pi, ti: (bi, pi, 0)),
            pl.BlockSpec((1, nb, width), lambda bi, pi, ti: (bi, 0, pi)),
            pl.BlockSpec((MOBA_HEADS_PER_STEP, NEAR_BLOCKS + 1, MOBA_BLOCK, MOBA_BLOCK),
                         lambda bi, pi, ti: (pi, 0, 0, 0), pipeline_mode=pl.Buffered(1)),
        ],
        out_specs=pl.BlockSpec((1, MOBA_BLOCK, width), lambda bi, pi, ti: (bi, ti, pi)),
        out_shape=jax.ShapeDtypeStruct((b, s, MOBA_WIDTH), F32),
        scratch_shapes=[pltpu.VMEM((MOBA_HEADS_PER_STEP, nb, MOBA_BLOCK), F32)],
        compiler_params=pltpu.CompilerParams(
            dimension_semantics=("arbitrary", "arbitrary", "arbitrary"), vmem_limit_bytes=VMEM_LIMIT_BYTES),
        name="moba_attention",
    )(rel_bias.astype(F32), q, k, vt, kmean, bias_tiles)


def _chunk_cumsum(g):
    groups = HGRN_CHUNK // SUBLANES
    g3 = g.reshape(groups, SUBLANES, g.shape[-1])
    sub = lax.broadcasted_iota(jnp.int32, g3.shape, 1)
    shift = 1
    while shift < SUBLANES:
        g3 = g3 + jnp.where(sub >= shift, pltpu.roll(g3, shift, axis=1), 0.0)
        shift *= 2
    rows = []
    run = None
    for i in range(groups):
        cur = g3[i] if run is None else g3[i] + run
        rows.append(cur)
        run = cur[SUBLANES - 1:SUBLANES, :]
    return jnp.concatenate(rows, axis=0)


def _hgrn_chunk(q, k, v, g, state_t):
    c = HGRN_CHUNK
    cum = _chunk_cumsum(g)
    last = cum[c - 1:c, :]
    v16 = v.astype(BF16)

    o = _dot_nt((q * jnp.exp(cum)).astype(BF16), state_t.astype(BF16))
    k_end = (k * jnp.exp(last - cum)).astype(BF16)
    new_state = state_t * jnp.exp(last) + _dot_tn(v16, k_end)

    row = lax.broadcasted_iota(jnp.int32, (c, 1), 0)
    rt = lax.broadcasted_iota(jnp.int32, (c, c), 0)
    cs = lax.broadcasted_iota(jnp.int32, (c, c), 1)
    scores = jnp.zeros((c, c), F32)

    for half in HGRN_LEVELS:
        refs = []
        for lo in range(0, c, 2 * half):
            r = lo + half - 1
            refs.append(jnp.broadcast_to(cum[r:r + 1, :], (2 * half, cum.shape[-1])))
        ref = jnp.concatenate(refs, axis=0)
        is_q = (row & half) != 0
        x = (jnp.where(is_q, q, k) * jnp.exp(jnp.where(is_q, cum - ref, ref - cum))).astype(BF16)
        block_scores = _dot_nt(x, x)
        shift = int(math.log2(2 * half))
        valid = ((rt >> shift) == (cs >> shift)) & ((rt & half) != 0) & ((cs & half) == 0)
        scores = scores + jnp.where(valid, block_scores, 0.0)

    groups = c // HGRN_DIAG
    cum3 = cum.reshape(groups, HGRN_DIAG, cum.shape[-1])
    q3 = q.reshape(groups, HGRN_DIAG, q.shape[-1])
    k3 = k.reshape(groups, HGRN_DIAG, k.shape[-1])
    for s in range(HGRN_DIAG):
        decay = jnp.exp(jnp.minimum(cum3 - cum3[:, s:s + 1, :], 0.0))
        col = jnp.sum(q3 * decay * k3[:, s:s + 1, :], axis=-1, keepdims=True).reshape(c, 1)
        here = (cs == (rt & ~(HGRN_DIAG - 1)) + s) & ((rt & (HGRN_DIAG - 1)) >= s)
        scores = jnp.where(here, col, scores)

    o = o + _dot(scores.astype(BF16), v16)
    return o, new_state


def _hgrn_kernel(q_ref, k_ref, v_ref, g_ref, out_ref, state_ref):
    @pl.when(pl.program_id(1) == 0)
    def _():
        state_ref[...] = jnp.zeros_like(state_ref)

    n_chunks = q_ref.shape[1] // HGRN_CHUNK

    def chunk(ci, carry):
        rows = pl.ds(pl.multiple_of(ci * HGRN_CHUNK, HGRN_CHUNK), HGRN_CHUNK)
        for h in range(HGRN_HEADS):
            cols = slice(h * HGRN_DIM, (h + 1) * HGRN_DIM)
            o, new_state = _hgrn_chunk(q_ref[0, rows, cols], k_ref[0, rows, cols], v_ref[0, rows, cols],
                                       g_ref[0, rows, cols], state_ref[h])
            out_ref[0, rows, cols] = o
            state_ref[h] = new_state
        return carry

    lax.fori_loop(0, n_chunks, chunk, 0)


def _hgrn(hq, hk, hv, hg):
    b, s, w = hq.shape
    ts = min(s, 512)
    spec = pl.BlockSpec((1, ts, w), lambda bi, ti: (bi, ti, 0))
    return pl.pallas_call(
        _hgrn_kernel,
        grid=(b, s // ts),
        in_specs=[spec, spec, spec, spec],
        out_specs=spec,
        out_shape=jax.ShapeDtypeStruct((b, s, w), F32),
        scratch_shapes=[pltpu.VMEM((HGRN_HEADS, HGRN_DIM, HGRN_DIM), F32)],
        compiler_params=pltpu.CompilerParams(
            dimension_semantics=("arbitrary", "arbitrary"), vmem_limit_bytes=VMEM_LIMIT_BYTES),
        name="hgrn2_recurrence",
    )(hq, hk, hv, hg)


def _merge_kernel(h_ref, ya_ref, ag_ref, yb_ref, bg_ref, ga_ref, gb_ref, p_ref, og_ref,
                  wa_ref, wb_ref, wo_ref, wp_ref, wg_ref, out_ref):
    ya = _dot((ya_ref[0] * ag_ref[0]).astype(BF16), wa_ref[0])

    yb = yb_ref[0]
    normed = []
    for h in range(HGRN_HEADS):
        t = yb[:, h * HGRN_DIM:(h + 1) * HGRN_DIM]
        normed.append(t * lax.rsqrt(jnp.mean(t * t, axis=-1, keepdims=True) + RMS_EPS))
    yb = jnp.concatenate(normed, axis=-1) * og_ref[...]
    yb = _dot((yb * bg_ref[0]).astype(BF16), wb_ref[0])

    merged = ga_ref[0] * ya + gb_ref[0] * yb
    h = h_ref[0] + _dot(merged.astype(BF16), wo_ref[0])
    ple = _dot(p_ref[0, 0].astype(BF16), wp_ref[0]) * _sigmoid(_dot(h.astype(BF16), wg_ref[0]))
    out_ref[0] = h + ple


def _merge(layer, h, ya, ag, yb, bg, ga, gb, p, out_gain, w_up_a, w_up_b, w_out, w_ple, w_ple_gate):
    b, s, d = h.shape
    tm = min(s, 512)
    tok = lambda width: pl.BlockSpec((1, tm, width), lambda bi, ti: (bi, ti, 0))
    wspec = lambda w: pl.BlockSpec((1,) + w.shape[1:], lambda bi, ti: (layer, 0, 0), pipeline_mode=pl.Buffered(1))
    return pl.pallas_call(
        _merge_kernel,
        grid=(b, s // tm),
        in_specs=[
            tok(d), tok(MOBA_WIDTH), tok(MOBA_WIDTH), tok(HGRN_WIDTH), tok(HGRN_WIDTH), tok(d), tok(d),
            pl.BlockSpec((1, 1, tm, p.shape[-1]), lambda bi, ti: (layer, bi, ti, 0)),
            pl.BlockSpec((1, HGRN_WIDTH), lambda bi, ti: (0, 0)),
            wspec(w_up_a), wspec(w_up_b), wspec(w_out), wspec(w_ple), wspec(w_ple_gate),
        ],
        out_specs=tok(d),
        out_shape=jax.ShapeDtypeStruct((b, s, d), F32),
        compiler_params=pltpu.CompilerParams(
            dimension_semantics=("arbitrary", "arbitrary"), vmem_limit_bytes=VMEM_LIMIT_BYTES),
        name=f"merge_l{layer}",
    )(h, ya, ag, yb, bg, ga, gb, p, out_gain, w_up_a, w_up_b, w_out, w_ple, w_ple_gate)


def kernel(x, p, norm_gain, w_in, q_norm_gain, k_norm_gain, rel_bias, hgrn_lb_logits, hgrn_out_gain,
           w_up_a, w_up_b, w_out, w_ple, w_ple_gate):
    b, s, d = x.shape
    depth = w_in.shape[0]
    assert s % MOBA_BLOCK == 0 and s % HGRN_CHUNK == 0

    w_in, w_up_a, w_up_b, w_out, w_ple, w_ple_gate = (
        w.astype(BF16) for w in (w_in, w_up_a, w_up_b, w_out, w_ple, w_ple_gate))
    head_id = jnp.arange(MOBA_WIDTH) // MOBA_HEAD_DIM
    hmean = ((head_id[:, None] == head_id[None, :]).astype(F32) / MOBA_HEAD_DIM).astype(BF16)
    tile = lambda g: jnp.tile(g.astype(F32), MOBA_HEADS)[None, :]

    bias_tiles = _bias_tiles(rel_bias)
    h = x
    for i in range(depth):
        q, k, vt, kmean, ag, hq, hk, hv, hg, bg, ga, gb = _in_proj(
            i, h, norm_gain[i][None, :].astype(F32), w_in, tile(q_norm_gain[i]), tile(k_norm_gain[i]),
            hgrn_lb_logits.astype(F32), hmean)
        ya = _moba(rel_bias, q, k, vt, kmean.reshape(b, s // MOBA_BLOCK, MOBA_WIDTH), bias_tiles)
        yb = _hgrn(hq, hk, hv, hg)
        h = _merge(i, h, ya, ag, yb, bg, ga, gb, p, hgrn_out_gain[i][None, :].astype(F32),
                   w_up_a, w_up_b, w_out, w_ple, w_ple_gate)
    return h
```

```python
import functools
import math

import jax
import jax.numpy as jnp
from jax import lax
from jax.experimental import pallas as pl
from jax.experimental.pallas import tpu as pltpu

F32 = jnp.float32
BF16 = jnp.bfloat16

MOBA_HEADS = 8
MOBA_HEAD_DIM = 64
MOBA_WIDTH = MOBA_HEADS * MOBA_HEAD_DIM
MOBA_BLOCK = 256
MOBA_TOPK = 3
HGRN_HEADS = 4
HGRN_DIM = 128
HGRN_WIDTH = HGRN_HEADS * HGRN_DIM
EXP_CLIP = 30.0
REL_BUCKETS = 32
REL_MAX_DIST = 2048
RMS_EPS = 1e-6
MASK_VALUE = -1e30

LANES = 128
SUBLANES = 8
VMEM_LIMIT_BYTES = 56 * 1024 * 1024

LOG2E = math.log2(math.e)
HEADS_PER_PAIR = LANES // MOBA_HEAD_DIM
MOBA_PAIRS_PER_STEP = 2
MOBA_HEADS_PER_STEP = HEADS_PER_PAIR * MOBA_PAIRS_PER_STEP
MOBA_GROUP = 2
MASK_LANES = 32
VT_ROWS = MOBA_HEAD_DIM + 16

_MAX_EXACT = REL_BUCKETS // 2
_LAST_BUCKET_DIST = math.ceil(_MAX_EXACT * (REL_MAX_DIST / _MAX_EXACT) ** ((REL_BUCKETS - 1 - _MAX_EXACT) / (REL_BUCKETS - _MAX_EXACT)))
NEAR_BLOCKS = -(-(_LAST_BUCKET_DIST + MOBA_BLOCK - 1) // MOBA_BLOCK)

HGRN_CHUNK = 64
HGRN_LEVELS = (32, 16, 8)
HGRN_DIAG = 8


def _sigmoid(x):
    return 1.0 / (1.0 + jnp.exp(-x))


def _silu(x):
    return x * _sigmoid(x)


def _dot(a, b):
    return jnp.dot(a, b, preferred_element_type=F32)


def _dot_nt(a, b):
    return lax.dot_general(a, b, (((1,), (1,)), ((), ())), preferred_element_type=F32)


def _dot_tn(a, b):
    return lax.dot_general(a, b, (((0,), (0,)), ((), ())), preferred_element_type=F32)


def _bias_tiles_kernel(tab_ref, out_ref):
    h = pl.program_id(0)
    d = pl.program_id(1)
    key = lax.broadcasted_iota(jnp.int32, (MOBA_BLOCK, MOBA_BLOCK), 0)
    qry = lax.broadcasted_iota(jnp.int32, (MOBA_BLOCK, MOBA_BLOCK), 1)
    rel = d * MOBA_BLOCK + qry - key
    n = jnp.maximum(rel, 0)
    nf = jnp.maximum(n, _MAX_EXACT).astype(F32)
    large = _MAX_EXACT + (jnp.log(nf / _MAX_EXACT) / math.log(REL_MAX_DIST / _MAX_EXACT)
                          * (REL_BUCKETS - _MAX_EXACT)).astype(jnp.int32)
    large = jnp.minimum(large, REL_BUCKETS - 1)
    bucket = jnp.where(n < _MAX_EXACT, n, large)
    val = jnp.zeros((MOBA_BLOCK, MOBA_BLOCK), F32)
    for b in range(REL_BUCKETS):
        val = jnp.where(bucket == b, tab_ref[b, h], val)
    val = jnp.where(rel >= 0, val * LOG2E, MASK_VALUE)
    out_ref[0, 0] = jnp.where(d < NEAR_BLOCKS, val, 0.0)


def _bias_tiles(rel_bias):
    return pl.pallas_call(
        _bias_tiles_kernel,
        grid=(MOBA_HEADS, NEAR_BLOCKS + 1),
        in_specs=[pl.BlockSpec(memory_space=pltpu.SMEM)],
        out_specs=pl.BlockSpec((1, 1, MOBA_BLOCK, MOBA_BLOCK), lambda h, d: (h, d, 0, 0)),
        out_shape=jax.ShapeDtypeStruct((MOBA_HEADS, NEAR_BLOCKS + 1, MOBA_BLOCK, MOBA_BLOCK), F32),
        name="t5_bias_tiles",
    )(rel_bias.astype(F32))


def _split_bf16(x):
    hi = x.astype(BF16)
    lo = (x - hi.astype(F32)).astype(BF16)
    return hi, lo


def _moba_queries(blk, qn, kmean, tab_ref, qaug_ref):
    tq = MOBA_BLOCK
    scale = MOBA_HEAD_DIM ** -0.5 * LOG2E
    qt = qn.T
    lane = lax.broadcasted_iota(jnp.int32, (MASK_LANES, LANES), 1)
    kblk = lax.broadcasted_iota(jnp.int32, (MASK_LANES, tq), 0)
    kblk_f = kblk.astype(F32)
    past = kblk < blk
    for pp in range(MOBA_HEADS // HEADS_PER_PAIR):
        qt_pair = qt[pp * LANES:(pp + 1) * LANES]
        q_hi, q_lo = _split_bf16(qt_pair)
        km_pair = kmean[:, pp * LANES:(pp + 1) * LANES]
        for hh in range(HEADS_PER_PAIR):
            h = pp * HEADS_PER_PAIR + hh
            in_head = (lane >= hh * MOBA_HEAD_DIM) & (lane < (hh + 1) * MOBA_HEAD_DIM)
            k_hi, k_lo = _split_bf16(jnp.where(in_head, km_pair, 0.0))
            gate = _dot(k_hi, q_hi) + _dot(k_hi, q_lo) + _dot(k_lo, q_hi)
            gate = jnp.where(past, gate, -jnp.inf)
            sel = jnp.zeros((MASK_LANES, tq), jnp.bool_)
            for _ in range(MOBA_TOPK):
                best = jnp.max(gate, axis=0, keepdims=True)
                first = jnp.min(jnp.where(gate == best, kblk_f, float(MASK_LANES)), axis=0, keepdims=True)
                hit = kblk_f == first
                sel = sel | hit
                gate = jnp.where(hit, -jnp.inf, gate)
            sel = (sel & past) | (kblk == blk)
            far_bias = jnp.where(blk - kblk >= NEAR_BLOCKS, tab_ref[REL_BUCKETS - 1, h] * LOG2E, 0.0)
            mask_hi, mask_lo = _split_bf16(jnp.where(sel, far_bias, MASK_VALUE))
            q_rows = (qt_pair[hh * MOBA_HEAD_DIM:(hh + 1) * MOBA_HEAD_DIM] * scale).astype(BF16)
            parts = [q_rows, mask_hi, mask_lo] if hh == 0 else [mask_hi, mask_lo, q_rows]
            qaug_ref[0, h, 0] = jnp.concatenate(parts, axis=0)


def _in_proj_kernel(layer, tab_ref, x_ref, gain_ref, w_ref, qg_ref, kg_ref, lbl_ref, hmean_ref,
                    qaug_ref, k_ref, vt_ref, ag_ref, hq_ref, hk_ref, hv_ref, hg_ref,
                    bg_ref, ga_ref, gb_ref, kmean_ref):
    blk = pl.program_id(1)
    x = x_ref[0]
    xn = x * lax.rsqrt(jnp.mean(x * x, axis=-1, keepdims=True) + RMS_EPS) * gain_ref[...]
    xn = xn.astype(BF16)

    def seg(start, width):
        return _dot(xn, w_ref[0, :, start:start + width])

    def head_rms(t, gain):
        ms = _dot((t * t).astype(BF16), hmean_ref[...])
        return t * lax.rsqrt(ms + RMS_EPS) * gain

    w = MOBA_WIDTH
    qn = head_rms(seg(0, w), qg_ref[...])
    kn = head_rms(seg(w, w), kg_ref[...])

    @pl.when(blk == 0)
    def _():
        kmean_ref[...] = jnp.zeros_like(kmean_ref)

    _moba_queries(blk, qn, kmean_ref[...], tab_ref, qaug_ref)
    kmean_ref[pl.ds(blk, 1), :] = jnp.mean(kn, axis=0, keepdims=True)
    lane = lax.broadcasted_iota(jnp.int32, (x.shape[0], LANES), 1)
    block_hot = jnp.where((lane & (MASK_LANES - 1)) == blk, 1.0, 0.0)
    for pp in range(MOBA_HEADS // HEADS_PER_PAIR):
        kp = kn[:, pp * LANES:(pp + 1) * LANES]
        k_ref[0, HEADS_PER_PAIR * pp] = jnp.where(lane < MOBA_HEAD_DIM, kp, block_hot).astype(BF16)
        k_ref[0, HEADS_PER_PAIR * pp + 1] = jnp.where(lane >= MOBA_HEAD_DIM, kp, block_hot).astype(BF16)
    vt = seg(2 * w, w).T
    extra = lax.broadcasted_iota(jnp.int32, (VT_ROWS - MOBA_HEAD_DIM, x.shape[0]), 0)
    for h in range(MOBA_HEADS):
        vt_ref[0, h, :MOBA_HEAD_DIM] = vt[h * MOBA_HEAD_DIM:(h + 1) * MOBA_HEAD_DIM].astype(BF16)
        vt_ref[0, h, MOBA_HEAD_DIM:] = jnp.where(extra == 0, 1.0, 0.0).astype(BF16)
    ag_ref[0] = _silu(seg(3 * w, w)).astype(BF16)

    base = 4 * w
    hw = HGRN_WIDTH
    hq_ref[0] = _silu(seg(base, hw))
    lbl = lbl_ref[...]
    e = jnp.exp(lbl - jnp.max(lbl, axis=0, keepdims=True))
    sm = e / jnp.sum(e, axis=0, keepdims=True)
    lb = jnp.zeros((1, hw), F32)
    for j in range(1, layer + 1):
        lb = lb + sm[j:j + 1, :]
    z = seg(base + hw, hw)
    log_sig = jnp.minimum(z, 0.0) - jnp.log1p(jnp.exp(-jnp.abs(z)))
    hg_ref[0] = (log_sig + jnp.log1p(lb * jnp.exp(jnp.minimum(-z, EXP_CLIP)))) * LOG2E
    hk_ref[0] = (1.0 - lb) * _sigmoid(-z)
    hv_ref[0] = seg(base + 2 * hw, hw)
    bg_ref[0] = _silu(seg(base + 3 * hw, hw)).astype(BF16)

    base = base + 4 * hw
    d = x.shape[-1]
    ga_ref[0] = _sigmoid(seg(base, d)).astype(BF16)
    gb_ref[0] = _sigmoid(seg(base + d, d)).astype(BF16)


def _in_proj(layer, rel_bias, h, norm_gain, w_in, q_gain, k_gain, lb_logits, hmean):
    b, s, d = h.shape
    depth = w_in.shape[0]
    tm = MOBA_BLOCK
    nt = s // tm
    assert nt <= MASK_LANES and HEADS_PER_PAIR == 2
    in_width = w_in.shape[-1]
    tok = lambda width: pl.BlockSpec((1, tm, width), lambda bi, ti: (bi, ti, 0))
    const2 = lambda shape: pl.BlockSpec(shape, lambda bi, ti: (0, 0))
    tok_shape = lambda width, dtype: jax.ShapeDtypeStruct((b, s, width), dtype)
    outs = (
        (pl.BlockSpec((1, MOBA_HEADS, 1, LANES, tm), lambda bi, ti: (bi, 0, ti, 0, 0)),
         jax.ShapeDtypeStruct((b, MOBA_HEADS, nt, LANES, tm), BF16)),
        (pl.BlockSpec((1, MOBA_HEADS, tm, LANES), lambda bi, ti: (bi, 0, ti, 0)),
         jax.ShapeDtypeStruct((b, MOBA_HEADS, s, LANES), BF16)),
        (pl.BlockSpec((1, MOBA_HEADS, VT_ROWS, tm), lambda bi, ti: (bi, 0, 0, ti)),
         jax.ShapeDtypeStruct((b, MOBA_HEADS, VT_ROWS, s), BF16)),
        (tok(MOBA_WIDTH), tok_shape(MOBA_WIDTH, BF16)),
        (tok(HGRN_WIDTH), tok_shape(HGRN_WIDTH, F32)),
        (tok(HGRN_WIDTH), tok_shape(HGRN_WIDTH, F32)),
        (tok(HGRN_WIDTH), tok_shape(HGRN_WIDTH, F32)),
        (tok(HGRN_WIDTH), tok_shape(HGRN_WIDTH, F32)),
        (tok(HGRN_WIDTH), tok_shape(HGRN_WIDTH, BF16)),
        (tok(d), tok_shape(d, BF16)),
        (tok(d), tok_shape(d, BF16)),
    )
    return pl.pallas_call(
        functools.partial(_in_proj_kernel, layer),
        grid=(b, nt),
        in_specs=[
            pl.BlockSpec(memory_space=pltpu.SMEM),
            tok(d),
            const2((1, d)),
            pl.BlockSpec((1, d, in_width), lambda bi, ti: (layer, 0, 0), pipeline_mode=pl.Buffered(1)),
            const2((1, MOBA_WIDTH)),
            const2((1, MOBA_WIDTH)),
            const2((depth, HGRN_WIDTH)),
            const2((MOBA_WIDTH, MOBA_WIDTH)),
        ],
        out_specs=[o[0] for o in outs],
        out_shape=[o[1] for o in outs],
        scratch_shapes=[pltpu.VMEM((MASK_LANES, MOBA_WIDTH), F32)],
        compiler_params=pltpu.CompilerParams(
            dimension_semantics=("arbitrary", "arbitrary"), vmem_limit_bytes=VMEM_LIMIT_BYTES),
        name=f"in_proj_l{layer}",
    )(rel_bias.astype(F32), h, norm_gain, w_in, q_gain, k_gain, lb_logits, hmean)


def _moba_kernel(qaug_ref, k_ref, vt_ref, bias_ref, out_ref, s0_ref, s1_ref):
    blk = pl.program_id(2)
    tq = MOBA_BLOCK
    group_rows = MOBA_GROUP * MOBA_BLOCK
    n_groups = k_ref.shape[2] // group_rows

    def score_group(g, s_ref):
        rows = pl.ds(pl.multiple_of(g * group_rows, group_rows), group_rows)
        for h in range(MOBA_HEADS_PER_STEP):
            s_ref[h] = _dot(k_ref[0, h, rows, :], qaug_ref[0, h, 0])

    def group(g, s_ref, state, near):
        rows = pl.ds(pl.multiple_of(g * group_rows, group_rows), group_rows)
        new = []
        for h in range(MOBA_HEADS_PER_STEP):
            m, acc = state[2 * h:2 * h + 2]
            s_blocks = []
            for u in range(MOBA_GROUP):
                s = s_ref[h, u * MOBA_BLOCK:(u + 1) * MOBA_BLOCK, :]
                if near:
                    dist = blk - (g * MOBA_GROUP + u)
                    s = s + bias_ref[h, jnp.where((dist >= 0) & (dist < NEAR_BLOCKS), dist, NEAR_BLOCKS)]
                s_blocks.append(s)
            s_max = s_blocks[0]
            for s in s_blocks[1:]:
                s_max = jnp.maximum(s_max, s)
            m_new = jnp.maximum(m, jnp.max(s_max, axis=0, keepdims=True))
            p_all = jnp.concatenate([jnp.exp2(s - m_new).astype(BF16) for s in s_blocks], axis=0)
            acc = jnp.exp2(m - m_new) * acc + _dot(vt_ref[0, h, :, rows], p_all)
            new += [m_new, acc]
        return new

    state = []
    for _ in range(MOBA_HEADS_PER_STEP):
        state += [jnp.full((1, tq), MASK_VALUE, F32), jnp.zeros((VT_ROWS, tq), F32)]
    def group_pair(i, state, near):
        score_group(2 * i + 1, s1_ref)
        state = group(2 * i, s0_ref, state, near)
        score_group(jnp.minimum(2 * i + 2, n_groups - 1), s0_ref)
        return group(2 * i + 1, s1_ref, state, near)

    pair_blocks = 2 * MOBA_GROUP
    n_far = jnp.maximum(blk - (NEAR_BLOCKS - 1), 0) // pair_blocks
    score_group(0, s0_ref)
    state = lax.fori_loop(0, n_far, lambda i, st: group_pair(i, st, False), state)
    state = lax.fori_loop(n_far, blk // pair_blocks + 1, lambda i, st: group_pair(i, st, True), state)

    accs = state[1::2]
    out_t = jnp.concatenate([a[:MOBA_HEAD_DIM] / a[MOBA_HEAD_DIM:MOBA_HEAD_DIM + 1] for a in accs], axis=0)
    out_ref[0] = out_t.T.astype(out_ref.dtype)


def _moba(qaug, k, vt, bias_tiles):
    b, _, s, _ = k.shape
    nb = s // MOBA_BLOCK
    assert nb % (2 * MOBA_GROUP) == 0
    hs = MOBA_HEADS_PER_STEP
    width = hs * MOBA_HEAD_DIM
    return pl.pallas_call(
        _moba_kernel,
        grid=(b, MOBA_HEADS // hs, nb),
        in_specs=[
            pl.BlockSpec((1, hs, 1, LANES, MOBA_BLOCK), lambda bi, pi, ti: (bi, pi, ti, 0, 0)),
            pl.BlockSpec((1, hs, s, LANES), lambda bi, pi, ti: (bi, pi, 0, 0)),
            pl.BlockSpec((1, hs, VT_ROWS, s), lambda bi, pi, ti: (bi, pi, 0, 0)),
            pl.BlockSpec((hs, NEAR_BLOCKS + 1, MOBA_BLOCK, MOBA_BLOCK),
                         lambda bi, pi, ti: (pi, 0, 0, 0), pipeline_mode=pl.Buffered(1)),
        ],
        out_specs=pl.BlockSpec((1, MOBA_BLOCK, width), lambda bi, pi, ti: (bi, ti, pi)),
        out_shape=jax.ShapeDtypeStruct((b, s, MOBA_WIDTH), BF16),
        scratch_shapes=[pltpu.VMEM((hs, MOBA_GROUP * MOBA_BLOCK, MOBA_BLOCK), F32),
                        pltpu.VMEM((hs, MOBA_GROUP * MOBA_BLOCK, MOBA_BLOCK), F32)],
        compiler_params=pltpu.CompilerParams(
            dimension_semantics=("arbitrary", "arbitrary", "arbitrary"), vmem_limit_bytes=VMEM_LIMIT_BYTES),
        name="moba_attention",
    )(qaug, k, vt, bias_tiles)


def _chunk_cumsum(g):
    groups = HGRN_CHUNK // SUBLANES
    g3 = g.reshape(groups, SUBLANES, g.shape[-1])
    sub = lax.broadcasted_iota(jnp.int32, g3.shape, 1)
    shift = 1
    while shift < SUBLANES:
        g3 = g3 + jnp.where(sub >= shift, pltpu.roll(g3, shift, axis=1), 0.0)
        shift *= 2
    rows = []
    run = None
    for i in range(groups):
        cur = g3[i] if run is None else g3[i] + run
        rows.append(cur)
        run = cur[SUBLANES - 1:SUBLANES, :]
    return jnp.concatenate(rows, axis=0)


def _hgrn_masks():
    c = HGRN_CHUNK
    row = lax.broadcasted_iota(jnp.int32, (c, 1), 0)
    rt = lax.broadcasted_iota(jnp.int32, (c, c), 0)
    cs = lax.broadcasted_iota(jnp.int32, (c, c), 1)
    levels = []
    for half in HGRN_LEVELS:
        shift = int(math.log2(2 * half))
        valid = ((rt >> shift) == (cs >> shift)) & ((rt & half) != 0) & ((cs & half) == 0)
        levels.append(((row & half) != 0, valid))
    diag = [(cs == (rt & ~(HGRN_DIAG - 1)) + s) & ((rt & (HGRN_DIAG - 1)) >= s) for s in range(HGRN_DIAG)]
    return levels, diag


def _hgrn_chunk_matmuls(q, k, v, g2, state_t, level_masks):
    c = HGRN_CHUNK
    cum = _chunk_cumsum(g2)
    last = cum[c - 1:c, :]
    v16 = v.astype(BF16)

    o = _dot_nt((q * jnp.exp2(cum)).astype(BF16), state_t.astype(BF16))
    k_end = (k * jnp.exp2(last - cum)).astype(BF16)
    new_state = state_t * jnp.exp2(last) + _dot_tn(v16, k_end)

    scores = jnp.zeros((c, c), F32)
    for half, (is_q, valid) in zip(HGRN_LEVELS, level_masks):
        refs = []
        for lo in range(0, c, 2 * half):
            r = lo + half - 1
            refs.append(jnp.broadcast_to(cum[r:r + 1, :], (2 * half, cum.shape[-1])))
        ref = jnp.concatenate(refs, axis=0)
        x = (jnp.where(is_q, q, k) * jnp.exp2(jnp.where(is_q, cum - ref, ref - cum))).astype(BF16)
        scores = scores + jnp.where(valid, _dot_nt(x, x), 0.0)
    return o, new_state, scores, cum, v16


def _hgrn_chunk_diag(q, k, cum, scores, diag_masks):
    c = HGRN_CHUNK
    groups = c // HGRN_DIAG
    cum3 = cum.reshape(groups, HGRN_DIAG, cum.shape[-1])
    q3 = q.reshape(groups, HGRN_DIAG, q.shape[-1])
    k3 = k.reshape(groups, HGRN_DIAG, k.shape[-1])
    for s in range(HGRN_DIAG):
        decay = jnp.exp2(cum3 - cum3[:, s:s + 1, :])
        col = jnp.sum(q3 * decay * k3[:, s:s + 1, :], axis=-1, keepdims=True).reshape(c, 1)
        scores = jnp.where(diag_masks[s], col, scores)
    return scores


def _hgrn_kernel(q_ref, k_ref, v_ref, g_ref, out_ref, state_ref):
    @pl.when(pl.program_id(1) == 0)
    def _():
        state_ref[...] = jnp.zeros_like(state_ref)

    n_chunks = q_ref.shape[1] // HGRN_CHUNK
    masks = _hgrn_masks()

    level_masks, diag_masks = masks

    def chunk(ci, carry):
        rows = pl.ds(pl.multiple_of(ci * HGRN_CHUNK, HGRN_CHUNK), HGRN_CHUNK)
        partial = []
        for h in range(HGRN_HEADS):
            cols = slice(h * HGRN_DIM, (h + 1) * HGRN_DIM)
            q, k = q_ref[0, rows, cols], k_ref[0, rows, cols]
            o, new_state, scores, cum, v16 = _hgrn_chunk_matmuls(
                q, k, v_ref[0, rows, cols], g_ref[0, rows, cols], state_ref[h], level_masks)
            state_ref[h] = new_state
            partial.append((q, k, o, scores, cum, v16))
        for h, (q, k, o, scores, cum, v16) in enumerate(partial):
            scores = _hgrn_chunk_diag(q, k, cum, scores, diag_masks)
            out = o + _dot(scores.astype(BF16), v16)
            out_ref[0, rows, h * HGRN_DIM:(h + 1) * HGRN_DIM] = out.astype(out_ref.dtype)
        return carry

    lax.fori_loop(0, n_chunks, chunk, 0, unroll=4)


def _hgrn(hq, hk, hv, hg):
    b, s, w = hq.shape
    ts = min(s, 512)
    spec = pl.BlockSpec((1, ts, w), lambda bi, ti: (bi, ti, 0))
    return pl.pallas_call(
        _hgrn_kernel,
        grid=(b, s // ts),
        in_specs=[spec, spec, spec, spec],
        out_specs=spec,
        out_shape=jax.ShapeDtypeStruct((b, s, w), BF16),
        scratch_shapes=[pltpu.VMEM((HGRN_HEADS, HGRN_DIM, HGRN_DIM), F32)],
        compiler_params=pltpu.CompilerParams(
            dimension_semantics=("arbitrary", "arbitrary"), vmem_limit_bytes=VMEM_LIMIT_BYTES),
        name="hgrn2_recurrence",
    )(hq, hk, hv, hg)


def _merge_kernel(h_ref, ya_ref, ag_ref, yb_ref, bg_ref, ga_ref, gb_ref, p_ref, og_ref,
                  wa_ref, wb_ref, wo_ref, wp_ref, wg_ref, out_ref):
    ya = _dot(ya_ref[0] * ag_ref[0], wa_ref[0])

    yb = yb_ref[0].astype(F32)
    normed = []
    for h in range(HGRN_HEADS):
        t = yb[:, h * HGRN_DIM:(h + 1) * HGRN_DIM]
        normed.append(t * lax.rsqrt(jnp.mean(t * t, axis=-1, keepdims=True) + RMS_EPS))
    yb = jnp.concatenate(normed, axis=-1) * og_ref[...]
    yb = _dot((yb * bg_ref[0]).astype(BF16), wb_ref[0])

    merged = ga_ref[0] * ya + gb_ref[0] * yb
    h = h_ref[0] + _dot(merged.astype(BF16), wo_ref[0])
    ple = _dot(p_ref[0, 0].astype(BF16), wp_ref[0]) * _sigmoid(_dot(h.astype(BF16), wg_ref[0]))
    out_ref[0] = h + ple


def _merge(layer, h, ya, ag, yb, bg, ga, gb, p, out_gain, w_up_a, w_up_b, w_out, w_ple, w_ple_gate):
    b, s, d = h.shape
    tm = min(s, 512)
    tok = lambda width: pl.BlockSpec((1, tm, width), lambda bi, ti: (bi, ti, 0))
    wspec = lambda w: pl.BlockSpec((1,) + w.shape[1:], lambda bi, ti: (layer, 0, 0), pipeline_mode=pl.Buffered(1))
    return pl.pallas_call(
        _merge_kernel,
        grid=(b, s // tm),
        in_specs=[
            tok(d), tok(MOBA_WIDTH), tok(MOBA_WIDTH), tok(HGRN_WIDTH), tok(HGRN_WIDTH), tok(d), tok(d),
            pl.BlockSpec((1, 1, tm, p.shape[-1]), lambda bi, ti: (layer, bi, ti, 0)),
            pl.BlockSpec((1, HGRN_WIDTH), lambda bi, ti: (0, 0)),
            wspec(w_up_a), wspec(w_up_b), wspec(w_out), wspec(w_ple), wspec(w_ple_gate),
        ],
        out_specs=tok(d),
        out_shape=jax.ShapeDtypeStruct((b, s, d), F32),
        compiler_params=pltpu.CompilerParams(
            dimension_semantics=("arbitrary", "arbitrary"), vmem_limit_bytes=VMEM_LIMIT_BYTES),
        name=f"merge_l{layer}",
    )(h, ya, ag, yb, bg, ga, gb, p, out_gain, w_up_a, w_up_b, w_out, w_ple, w_ple_gate)


def kernel(x, p, norm_gain, w_in, q_norm_gain, k_norm_gain, rel_bias, hgrn_lb_logits, hgrn_out_gain,
           w_up_a, w_up_b, w_out, w_ple, w_ple_gate):
    b, s, d = x.shape
    depth = w_in.shape[0]
    assert s % MOBA_BLOCK == 0 and s % HGRN_CHUNK == 0

    w_in, w_up_a, w_up_b, w_out, w_ple, w_ple_gate = (
        w.astype(BF16) for w in (w_in, w_up_a, w_up_b, w_out, w_ple, w_ple_gate))
    head_id = jnp.arange(MOBA_WIDTH) // MOBA_HEAD_DIM
    hmean = ((head_id[:, None] == head_id[None, :]).astype(F32) / MOBA_HEAD_DIM).astype(BF16)
    tile = lambda g: jnp.tile(g.astype(F32), MOBA_HEADS)[None, :]

    bias_tiles = _bias_tiles(rel_bias)
    h = x
    for i in range(depth):
        qaug, k, vt, ag, hq, hk, hv, hg, bg, ga, gb = _in_proj(
            i, rel_bias, h, norm_gain[i][None, :].astype(F32), w_in, tile(q_norm_gain[i]), tile(k_norm_gain[i]),
            hgrn_lb_logits.astype(F32), hmean)
        ya = _moba(qaug, k, vt, bias_tiles)
        yb = _hgrn(hq, hk, hv, hg)
        h = _merge(i, h, ya, ag, yb, bg, ga, gb, p, hgrn_out_gain[i][None, :].astype(F32),
                   w_up_a, w_up_b, w_out, w_ple, w_ple_gate)
    return h
```

```python
import functools
import math

import jax
import jax.numpy as jnp
from jax import lax
from jax.experimental import pallas as pl
from jax.experimental.pallas import tpu as pltpu

F32 = jnp.float32
BF16 = jnp.bfloat16

MOBA_HEADS = 8
MOBA_HEAD_DIM = 64
MOBA_WIDTH = MOBA_HEADS * MOBA_HEAD_DIM
MOBA_BLOCK = 256
MOBA_TOPK = 3
HGRN_HEADS = 4
HGRN_DIM = 128
HGRN_WIDTH = HGRN_HEADS * HGRN_DIM
EXP_CLIP = 30.0
REL_BUCKETS = 32
REL_MAX_DIST = 2048
RMS_EPS = 1e-6
MASK_VALUE = -1e30

LANES = 128
SUBLANES = 8
VMEM_LIMIT_BYTES = 56 * 1024 * 1024

LOG2E = math.log2(math.e)
HEADS_PER_PAIR = LANES // MOBA_HEAD_DIM
MOBA_PAIRS_PER_STEP = 2
MOBA_HEADS_PER_STEP = HEADS_PER_PAIR * MOBA_PAIRS_PER_STEP
MOBA_GROUP = 2
MASK_LANES = 32
PROJ_CHUNK = 512
PROJ_LOOKAHEAD = 1024
VT_ROWS = MOBA_HEAD_DIM + 16

_MAX_EXACT = REL_BUCKETS // 2
_LAST_BUCKET_DIST = math.ceil(_MAX_EXACT * (REL_MAX_DIST / _MAX_EXACT) ** ((REL_BUCKETS - 1 - _MAX_EXACT) / (REL_BUCKETS - _MAX_EXACT)))
NEAR_BLOCKS = -(-(_LAST_BUCKET_DIST + MOBA_BLOCK - 1) // MOBA_BLOCK)

HGRN_CHUNK = 64
HGRN_LEVELS = (32, 16, 8, 4, 2, 1)


def _sigmoid(x):
    return 1.0 / (1.0 + jnp.exp(-x))


def _silu(x):
    return x * _sigmoid(x)


def _dot(a, b):
    return jnp.dot(a, b, preferred_element_type=F32)


def _dot_nt(a, b):
    return lax.dot_general(a, b, (((1,), (1,)), ((), ())), preferred_element_type=F32)


def _dot_tn(a, b):
    return lax.dot_general(a, b, (((0,), (0,)), ((), ())), preferred_element_type=F32)


def _bias_tiles_kernel(tab_ref, out_ref):
    h = pl.program_id(0)
    d = pl.program_id(1)
    key = lax.broadcasted_iota(jnp.int32, (MOBA_BLOCK, MOBA_BLOCK), 0)
    qry = lax.broadcasted_iota(jnp.int32, (MOBA_BLOCK, MOBA_BLOCK), 1)
    rel = d * MOBA_BLOCK + qry - key
    n = jnp.maximum(rel, 0)
    nf = jnp.maximum(n, _MAX_EXACT).astype(F32)
    large = _MAX_EXACT + (jnp.log(nf / _MAX_EXACT) / math.log(REL_MAX_DIST / _MAX_EXACT)
                          * (REL_BUCKETS - _MAX_EXACT)).astype(jnp.int32)
    large = jnp.minimum(large, REL_BUCKETS - 1)
    bucket = jnp.where(n < _MAX_EXACT, n, large)
    val = jnp.zeros((MOBA_BLOCK, MOBA_BLOCK), F32)
    for b in range(REL_BUCKETS):
        val = jnp.where(bucket == b, tab_ref[b, h], val)
    val = jnp.where(rel >= 0, val * LOG2E, MASK_VALUE)
    out_ref[0, 0] = jnp.where(d < NEAR_BLOCKS, val, 0.0)


def _bias_tiles(rel_bias):
    return pl.pallas_call(
        _bias_tiles_kernel,
        grid=(MOBA_HEADS, NEAR_BLOCKS + 1),
        in_specs=[pl.BlockSpec(memory_space=pltpu.SMEM)],
        out_specs=pl.BlockSpec((1, 1, MOBA_BLOCK, MOBA_BLOCK), lambda h, d: (h, d, 0, 0)),
        out_shape=jax.ShapeDtypeStruct((MOBA_HEADS, NEAR_BLOCKS + 1, MOBA_BLOCK, MOBA_BLOCK), F32),
        name="t5_bias_tiles",
    )(rel_bias.astype(F32))


def _split_bf16(x):
    hi = x.astype(BF16)
    lo = (x - hi.astype(F32)).astype(BF16)
    return hi, lo


def _moba_queries(blk, qn, kmean, tab_ref, qaug_ref):
    tq = MOBA_BLOCK
    scale = MOBA_HEAD_DIM ** -0.5 * LOG2E
    qt = qn.T
    lane = lax.broadcasted_iota(jnp.int32, (MASK_LANES, LANES), 1)
    kblk = lax.broadcasted_iota(jnp.int32, (MASK_LANES, tq), 0)
    kblk_f = kblk.astype(F32)
    past = kblk < blk
    for pp in range(MOBA_HEADS // HEADS_PER_PAIR):
        qt_pair = qt[pp * LANES:(pp + 1) * LANES]
        q_hi, q_lo = _split_bf16(qt_pair)
        km_pair = kmean[:, pp * LANES:(pp + 1) * LANES]
        for hh in range(HEADS_PER_PAIR):
            h = pp * HEADS_PER_PAIR + hh
            in_head = (lane >= hh * MOBA_HEAD_DIM) & (lane < (hh + 1) * MOBA_HEAD_DIM)
            k_hi, k_lo = _split_bf16(jnp.where(in_head, km_pair, 0.0))
            gate = _dot(k_hi, q_hi) + _dot(k_hi, q_lo) + _dot(k_lo, q_hi)
            gate = jnp.where(past, gate, -jnp.inf)
            sel = jnp.zeros((MASK_LANES, tq), jnp.bool_)
            for _ in range(MOBA_TOPK):
                best = jnp.max(gate, axis=0, keepdims=True)
                first = jnp.min(jnp.where(gate == best, kblk_f, float(MASK_LANES)), axis=0, keepdims=True)
                hit = kblk_f == first
                sel = sel | hit
                gate = jnp.where(hit, -jnp.inf, gate)
            sel = (sel & past) | (kblk == blk)
            far_bias = jnp.where(blk - kblk >= NEAR_BLOCKS, tab_ref[REL_BUCKETS - 1, h] * LOG2E, 0.0)
            mask_hi, mask_lo = _split_bf16(jnp.where(sel, far_bias, MASK_VALUE))
            q_rows = (qt_pair[hh * MOBA_HEAD_DIM:(hh + 1) * MOBA_HEAD_DIM] * scale).astype(BF16)
            parts = [q_rows, mask_hi, mask_lo] if hh == 0 else [mask_hi, mask_lo, q_rows]
            qaug_ref[0, h, 0] = jnp.concatenate(parts, axis=0)


def _in_proj_kernel(layer, tab_ref, x_ref, gain_ref, w_ref, qg_ref, kg_ref, lbl_ref, hmean_ref,
                    qaug_ref, k_ref, vt_ref, ag_ref, hq_ref, hk_ref, hv_ref, hg_ref,
                    bg_ref, ga_ref, gb_ref, kmean_ref, proj_ref):
    blk = pl.program_id(1)
    x = x_ref[0]
    xn = x * lax.rsqrt(jnp.mean(x * x, axis=-1, keepdims=True) + RMS_EPS) * gain_ref[...]
    xn = xn.astype(BF16)

    in_width = w_ref.shape[-1]
    issued = [0]

    def seg(start, width):
        while issued[0] < min(start + width + PROJ_LOOKAHEAD, in_width):
            c0 = issued[0]
            proj_ref[:, c0:c0 + PROJ_CHUNK] = _dot(xn, w_ref[0, :, c0:c0 + PROJ_CHUNK])
            issued[0] = c0 + PROJ_CHUNK
        return proj_ref[:, start:start + width]

    def head_rms(t, gain):
        ms = _dot((t * t).astype(BF16), hmean_ref[...])
        return t * lax.rsqrt(ms + RMS_EPS) * gain

    w = MOBA_WIDTH
    qn = head_rms(seg(0, w), qg_ref[...])
    kn = head_rms(seg(w, w), kg_ref[...])

    @pl.when(blk == 0)
    def _():
        kmean_ref[...] = jnp.zeros_like(kmean_ref)

    _moba_queries(blk, qn, kmean_ref[...], tab_ref, qaug_ref)
    kmean_ref[pl.ds(blk, 1), :] = jnp.mean(kn, axis=0, keepdims=True)
    lane = lax.broadcasted_iota(jnp.int32, (x.shape[0], LANES), 1)
    block_hot = jnp.where((lane & (MASK_LANES - 1)) == blk, 1.0, 0.0)
    for pp in range(MOBA_HEADS // HEADS_PER_PAIR):
        kp = kn[:, pp * LANES:(pp + 1) * LANES]
        k_ref[0, HEADS_PER_PAIR * pp] = jnp.where(lane < MOBA_HEAD_DIM, kp, block_hot).astype(BF16)
        k_ref[0, HEADS_PER_PAIR * pp + 1] = jnp.where(lane >= MOBA_HEAD_DIM, kp, block_hot).astype(BF16)
    vt = seg(2 * w, w).T
    extra = lax.broadcasted_iota(jnp.int32, (VT_ROWS - MOBA_HEAD_DIM, x.shape[0]), 0)
    for h in range(MOBA_HEADS):
        vt_ref[0, h, :MOBA_HEAD_DIM] = vt[h * MOBA_HEAD_DIM:(h + 1) * MOBA_HEAD_DIM].astype(BF16)
        vt_ref[0, h, MOBA_HEAD_DIM:] = jnp.where(extra == 0, 1.0, 0.0).astype(BF16)
    ag_ref[0] = _silu(seg(3 * w, w)).astype(BF16)

    base = 4 * w
    hw = HGRN_WIDTH
    hq_ref[0] = _silu(seg(base, hw))
    lbl = lbl_ref[...]
    e = jnp.exp(lbl - jnp.max(lbl, axis=0, keepdims=True))
    sm = e / jnp.sum(e, axis=0, keepdims=True)
    lb = jnp.zeros((1, hw), F32)
    for j in range(1, layer + 1):
        lb = lb + sm[j:j + 1, :]
    z = seg(base + hw, hw)
    e = jnp.exp(-jnp.abs(z))
    pos = z >= 0.0
    inv = 1.0 / (1.0 + e)
    clipped = jnp.where(pos, e, jnp.minimum(1.0 / e, math.exp(EXP_CLIP)))
    hg_ref[0] = jnp.minimum(z, 0.0) * LOG2E + jnp.log2((1.0 + lb * clipped) * inv)
    hk_ref[0] = (1.0 - lb) * jnp.where(pos, e, 1.0) * inv
    hv_ref[0] = seg(base + 2 * hw, hw)
    bg_ref[0] = _silu(seg(base + 3 * hw, hw)).astype(BF16)

    base = base + 4 * hw
    d = x.shape[-1]
    ga_ref[0] = _sigmoid(seg(base, d)).astype(BF16)
    gb_ref[0] = _sigmoid(seg(base + d, d)).astype(BF16)


def _in_proj(layer, rel_bias, h, norm_gain, w_in, q_gain, k_gain, lb_logits, hmean):
    b, s, d = h.shape
    depth = w_in.shape[0]
    tm = MOBA_BLOCK
    nt = s // tm
    assert nt <= MASK_LANES and HEADS_PER_PAIR == 2
    in_width = w_in.shape[-1]
    tok = lambda width: pl.BlockSpec((1, tm, width), lambda bi, ti: (bi, ti, 0))
    const2 = lambda shape: pl.BlockSpec(shape, lambda bi, ti: (0, 0))
    tok_shape = lambda width, dtype: jax.ShapeDtypeStruct((b, s, width), dtype)
    outs = (
        (pl.BlockSpec((1, MOBA_HEADS, 1, LANES, tm), lambda bi, ti: (bi, 0, ti, 0, 0)),
         jax.ShapeDtypeStruct((b, MOBA_HEADS, nt, LANES, tm), BF16)),
        (pl.BlockSpec((1, MOBA_HEADS, tm, LANES), lambda bi, ti: (bi, 0, ti, 0)),
         jax.ShapeDtypeStruct((b, MOBA_HEADS, s, LANES), BF16)),
        (pl.BlockSpec((1, MOBA_HEADS, VT_ROWS, tm), lambda bi, ti: (bi, 0, 0, ti)),
         jax.ShapeDtypeStruct((b, MOBA_HEADS, VT_ROWS, s), BF16)),
        (tok(MOBA_WIDTH), tok_shape(MOBA_WIDTH, BF16)),
        (tok(HGRN_WIDTH), tok_shape(HGRN_WIDTH, F32)),
        (tok(HGRN_WIDTH), tok_shape(HGRN_WIDTH, F32)),
        (tok(HGRN_WIDTH), tok_shape(HGRN_WIDTH, F32)),
        (tok(HGRN_WIDTH), tok_shape(HGRN_WIDTH, F32)),
        (tok(HGRN_WIDTH), tok_shape(HGRN_WIDTH, BF16)),
        (tok(d), tok_shape(d, BF16)),
        (tok(d), tok_shape(d, BF16)),
    )
    return pl.pallas_call(
        functools.partial(_in_proj_kernel, layer),
        grid=(b, nt),
        in_specs=[
            pl.BlockSpec(memory_space=pltpu.SMEM),
            tok(d),
            const2((1, d)),
            pl.BlockSpec((1, d, in_width), lambda bi, ti: (layer, 0, 0), pipeline_mode=pl.Buffered(1)),
            const2((1, MOBA_WIDTH)),
            const2((1, MOBA_WIDTH)),
            const2((depth, HGRN_WIDTH)),
            const2((MOBA_WIDTH, MOBA_WIDTH)),
        ],
        out_specs=[o[0] for o in outs],
        out_shape=[o[1] for o in outs],
        scratch_shapes=[pltpu.VMEM((MASK_LANES, MOBA_WIDTH), F32), pltpu.VMEM((tm, in_width), F32)],
        compiler_params=pltpu.CompilerParams(
            dimension_semantics=("arbitrary", "arbitrary"), vmem_limit_bytes=VMEM_LIMIT_BYTES),
        name=f"in_proj_l{layer}",
    )(rel_bias.astype(F32), h, norm_gain, w_in, q_gain, k_gain, lb_logits, hmean)


def _moba_kernel(qaug_ref, qnext_ref, k_ref, vt_ref, bias_ref, out_ref, s0_ref, s1_ref):
    blk = pl.program_id(2)
    tq = MOBA_BLOCK
    group_rows = MOBA_GROUP * MOBA_BLOCK
    n_groups = k_ref.shape[2] // group_rows

    def score_group(g, s_ref, q_ref=qaug_ref):
        rows = pl.ds(pl.multiple_of(g * group_rows, group_rows), group_rows)
        for h in range(MOBA_HEADS_PER_STEP):
            s_ref[h] = _dot(k_ref[0, h, rows, :], q_ref[0, h, 0])

    def group(g, s_ref, state, near):
        rows = pl.ds(pl.multiple_of(g * group_rows, group_rows), group_rows)
        new = []
        for h in range(MOBA_HEADS_PER_STEP):
            m, acc = state[2 * h:2 * h + 2]
            s_blocks = []
            for u in range(MOBA_GROUP):
                s = s_ref[h, u * MOBA_BLOCK:(u + 1) * MOBA_BLOCK, :]
                if near:
                    dist = blk - (g * MOBA_GROUP + u)
                    s = s + bias_ref[h, jnp.where((dist >= 0) & (dist < NEAR_BLOCKS), dist, NEAR_BLOCKS)]
                s_blocks.append(s)
            s_max = s_blocks[0]
            for s in s_blocks[1:]:
                s_max = jnp.maximum(s_max, s)
            m_new = jnp.maximum(m, jnp.max(s_max, axis=0, keepdims=True))
            p_all = jnp.concatenate([jnp.exp2(s - m_new).astype(BF16) for s in s_blocks], axis=0)
            acc = jnp.exp2(m - m_new) * acc + _dot(vt_ref[0, h, :, rows], p_all)
            new += [m_new, acc]
        return new

    state = []
    for _ in range(MOBA_HEADS_PER_STEP):
        state += [jnp.full((1, tq), MASK_VALUE, F32), jnp.zeros((VT_ROWS, tq), F32)]
    def group_pair(i, state, near):
        score_group(2 * i + 1, s1_ref)
        state = group(2 * i, s0_ref, state, near)
        score_group(jnp.minimum(2 * i + 2, n_groups - 1), s0_ref)
        return group(2 * i + 1, s1_ref, state, near)

    pair_blocks = 2 * MOBA_GROUP
    n_far = jnp.maximum(blk - (NEAR_BLOCKS - 1), 0) // pair_blocks
    last = blk // pair_blocks

    @pl.when(blk == 0)
    def _():
        score_group(0, s0_ref)

    state = lax.fori_loop(0, n_far, lambda i, st: group_pair(i, st, False), state)
    state = lax.fori_loop(n_far, last, lambda i, st: group_pair(i, st, True), state)

    second = (blk % pair_blocks) >= MOBA_GROUP
    score_group(2 * last + 1, s1_ref)
    state = group(2 * last, s0_ref, state, True)
    score_group(0, s0_ref, qnext_ref)
    state = lax.cond(second, lambda st: tuple(group(2 * last + 1, s1_ref, list(st), True)),
                     lambda st: tuple(st), tuple(state))

    accs = state[1::2]
    out_t = jnp.concatenate([a[:MOBA_HEAD_DIM] / a[MOBA_HEAD_DIM:MOBA_HEAD_DIM + 1] for a in accs], axis=0)
    out_ref[0] = out_t.T.astype(out_ref.dtype)


def _moba(qaug, k, vt, bias_tiles):
    b, _, s, _ = k.shape
    nb = s // MOBA_BLOCK
    assert nb % (2 * MOBA_GROUP) == 0
    hs = MOBA_HEADS_PER_STEP
    width = hs * MOBA_HEAD_DIM
    return pl.pallas_call(
        _moba_kernel,
        grid=(b, MOBA_HEADS // hs, nb),
        in_specs=[
            pl.BlockSpec((1, hs, 1, LANES, MOBA_BLOCK), lambda bi, pi, ti: (bi, pi, ti, 0, 0)),
            pl.BlockSpec((1, hs, 1, LANES, MOBA_BLOCK),
                         lambda bi, pi, ti: (bi, pi, jnp.minimum(ti + 1, nb - 1), 0, 0)),
            pl.BlockSpec((1, hs, s, LANES), lambda bi, pi, ti: (bi, pi, 0, 0)),
            pl.BlockSpec((1, hs, VT_ROWS, s), lambda bi, pi, ti: (bi, pi, 0, 0)),
            pl.BlockSpec((hs, NEAR_BLOCKS + 1, MOBA_BLOCK, MOBA_BLOCK),
                         lambda bi, pi, ti: (pi, 0, 0, 0), pipeline_mode=pl.Buffered(1)),
        ],
        out_specs=pl.BlockSpec((1, MOBA_BLOCK, width), lambda bi, pi, ti: (bi, ti, pi)),
        out_shape=jax.ShapeDtypeStruct((b, s, MOBA_WIDTH), BF16),
        scratch_shapes=[pltpu.VMEM((hs, MOBA_GROUP * MOBA_BLOCK, MOBA_BLOCK), F32),
                        pltpu.VMEM((hs, MOBA_GROUP * MOBA_BLOCK, MOBA_BLOCK), F32)],
        compiler_params=pltpu.CompilerParams(
            dimension_semantics=("arbitrary", "arbitrary", "arbitrary"), vmem_limit_bytes=VMEM_LIMIT_BYTES),
        name="moba_attention",
    )(qaug, qaug, k, vt, bias_tiles)


def _chunk_cumsum(g):
    groups = HGRN_CHUNK // SUBLANES
    g3 = g.reshape(groups, SUBLANES, g.shape[-1])
    sub = lax.broadcasted_iota(jnp.int32, g3.shape, 1)
    shift = 1
    while shift < SUBLANES:
        g3 = g3 + jnp.where(sub >= shift, pltpu.roll(g3, shift, axis=1), 0.0)
        shift *= 2
    rows = []
    run = None
    for i in range(groups):
        cur = g3[i] if run is None else g3[i] + run
        rows.append(cur)
        run = cur[SUBLANES - 1:SUBLANES, :]
    return jnp.concatenate(rows, axis=0)


def _hgrn_masks():
    c = HGRN_CHUNK
    row = lax.broadcasted_iota(jnp.int32, (c, 1), 0)
    rt = lax.broadcasted_iota(jnp.int32, (c, c), 0)
    cs = lax.broadcasted_iota(jnp.int32, (c, c), 1)
    levels = []
    for half in HGRN_LEVELS:
        shift = int(math.log2(2 * half))
        valid = ((rt >> shift) == (cs >> shift)) & ((rt & half) != 0) & ((cs & half) == 0)
        levels.append(((row & half) != 0, valid))
    return levels, rt == cs


def _hgrn_chunk_matmuls(q, k, v, g2, state_t, level_masks):
    c = HGRN_CHUNK
    cum = _chunk_cumsum(g2)
    last = cum[c - 1:c, :]
    v16 = v.astype(BF16)

    o = _dot_nt((q * jnp.exp2(cum)).astype(BF16), state_t.astype(BF16))
    k_end = (k * jnp.exp2(last - cum)).astype(BF16)
    new_state = state_t * jnp.exp2(last) + _dot_tn(v16, k_end)

    scores = jnp.zeros((c, c), F32)
    groups = c // SUBLANES
    width = cum.shape[-1]
    cum3 = cum.reshape(groups, SUBLANES, width)
    sub = lax.broadcasted_iota(jnp.int32, (groups, SUBLANES, width), 1)

    def row_of_group(r):
        return jnp.broadcast_to(cum3[:, r:r + 1, :], cum3.shape)

    for half, (is_q, valid) in zip(HGRN_LEVELS, level_masks):
        if half == 1:
            exponent = jnp.where(is_q, g2, 0.0)
        else:
            if half >= SUBLANES:
                ref = jnp.concatenate(
                    [jnp.broadcast_to(cum[lo + half - 1:lo + half, :], (2 * half, width))
                     for lo in range(0, c, 2 * half)], axis=0)
            elif half == 4:
                ref = row_of_group(3).reshape(c, width)
            else:
                ref = jnp.where(sub < 4, row_of_group(1), row_of_group(5)).reshape(c, width)
            exponent = jnp.where(is_q, cum - ref, ref - cum)
        x = (jnp.where(is_q, q, k) * jnp.exp2(exponent)).astype(BF16)
        scores = scores + jnp.where(valid, _dot_nt(x, x), 0.0)
    return o, new_state, scores, v16


def _hgrn_kernel(q_ref, k_ref, v_ref, g_ref, out_ref, state_ref):
    @pl.when(pl.program_id(1) == 0)
    def _():
        state_ref[...] = jnp.zeros_like(state_ref)

    n_chunks = q_ref.shape[1] // HGRN_CHUNK
    masks = _hgrn_masks()

    level_masks, on_diagonal = masks

    def chunk(ci, carry):
        rows = pl.ds(pl.multiple_of(ci * HGRN_CHUNK, HGRN_CHUNK), HGRN_CHUNK)
        partial = []
        for h in range(HGRN_HEADS):
            cols = slice(h * HGRN_DIM, (h + 1) * HGRN_DIM)
            q, k = q_ref[0, rows, cols], k_ref[0, rows, cols]
            o, new_state, scores, v16 = _hgrn_chunk_matmuls(
                q, k, v_ref[0, rows, cols], g_ref[0, rows, cols], state_ref[h], level_masks)
            state_ref[h] = new_state
            partial.append((jnp.sum(q * k, axis=-1, keepdims=True), o, scores, v16))
        for h, (self_score, o, scores, v16) in enumerate(partial):
            scores = jnp.where(on_diagonal, self_score, scores)
            out = o + _dot(scores.astype(BF16), v16)
            out_ref[0, rows, h * HGRN_DIM:(h + 1) * HGRN_DIM] = out.astype(out_ref.dtype)
        return carry

    lax.fori_loop(0, n_chunks, chunk, 0, unroll=4)


def _hgrn(hq, hk, hv, hg):
    b, s, w = hq.shape
    ts = min(s, 512)
    spec = pl.BlockSpec((1, ts, w), lambda bi, ti: (bi, ti, 0))
    return pl.pallas_call(
        _hgrn_kernel,
        grid=(b, s // ts),
        in_specs=[spec, spec, spec, spec],
        out_specs=spec,
        out_shape=jax.ShapeDtypeStruct((b, s, w), BF16),
        scratch_shapes=[pltpu.VMEM((HGRN_HEADS, HGRN_DIM, HGRN_DIM), F32)],
        compiler_params=pltpu.CompilerParams(
            dimension_semantics=("arbitrary", "arbitrary"), vmem_limit_bytes=VMEM_LIMIT_BYTES),
        name="hgrn2_recurrence",
    )(hq, hk, hv, hg)


def _merge_kernel(h_ref, ya_ref, ag_ref, yb_ref, bg_ref, ga_ref, gb_ref, p_ref, og_ref,
                  wa_ref, wb_ref, wo_ref, wp_ref, wg_ref, out_ref):
    ya = _dot(ya_ref[0] * ag_ref[0], wa_ref[0])

    yb = yb_ref[0].astype(F32)
    normed = []
    for h in range(HGRN_HEADS):
        t = yb[:, h * HGRN_DIM:(h + 1) * HGRN_DIM]
        normed.append(t * lax.rsqrt(jnp.mean(t * t, axis=-1, keepdims=True) + RMS_EPS))
    yb = jnp.concatenate(normed, axis=-1) * og_ref[...]
    yb = _dot((yb * bg_ref[0]).astype(BF16), wb_ref[0])

    merged = ga_ref[0] * ya + gb_ref[0] * yb
    h = h_ref[0] + _dot(merged.astype(BF16), wo_ref[0])
    ple = _dot(p_ref[0, 0].astype(BF16), wp_ref[0]) * _sigmoid(_dot(h.astype(BF16), wg_ref[0]))
    out_ref[0] = h + ple


def _merge(layer, h, ya, ag, yb, bg, ga, gb, p, out_gain, w_up_a, w_up_b, w_out, w_ple, w_ple_gate):
    b, s, d = h.shape
    tm = min(s, 512)
    tok = lambda width: pl.BlockSpec((1, tm, width), lambda bi, ti: (bi, ti, 0))
    wspec = lambda w: pl.BlockSpec((1,) + w.shape[1:], lambda bi, ti: (layer, 0, 0), pipeline_mode=pl.Buffered(1))
    return pl.pallas_call(
        _merge_kernel,
        grid=(b, s // tm),
        in_specs=[
            tok(d), tok(MOBA_WIDTH), tok(MOBA_WIDTH), tok(HGRN_WIDTH), tok(HGRN_WIDTH), tok(d), tok(d),
            pl.BlockSpec((1, 1, tm, p.shape[-1]), lambda bi, ti: (layer, bi, ti, 0)),
            pl.BlockSpec((1, HGRN_WIDTH), lambda bi, ti: (0, 0)),
            wspec(w_up_a), wspec(w_up_b), wspec(w_out), wspec(w_ple), wspec(w_ple_gate),
        ],
        out_specs=tok(d),
        out_shape=jax.ShapeDtypeStruct((b, s, d), F32),
        compiler_params=pltpu.CompilerParams(
            dimension_semantics=("arbitrary", "arbitrary"), vmem_limit_bytes=VMEM_LIMIT_BYTES),
        name=f"merge_l{layer}",
    )(h, ya, ag, yb, bg, ga, gb, p, out_gain, w_up_a, w_up_b, w_out, w_ple, w_ple_gate)


def kernel(x, p, norm_gain, w_in, q_norm_gain, k_norm_gain, rel_bias, hgrn_lb_logits, hgrn_out_gain,
           w_up_a, w_up_b, w_out, w_ple, w_ple_gate):
    b, s, d = x.shape
    depth = w_in.shape[0]
    assert s % MOBA_BLOCK == 0 and s % HGRN_CHUNK == 0

    w_in, w_up_a, w_up_b, w_out, w_ple, w_ple_gate = (
        w.astype(BF16) for w in (w_in, w_up_a, w_up_b, w_out, w_ple, w_ple_gate))
    head_id = jnp.arange(MOBA_WIDTH) // MOBA_HEAD_DIM
    hmean = ((head_id[:, None] == head_id[None, :]).astype(F32) / MOBA_HEAD_DIM).astype(BF16)
    tile = lambda g: jnp.tile(g.astype(F32), MOBA_HEADS)[None, :]

    bias_tiles = _bias_tiles(rel_bias)
    h = x
    for i in range(depth):
        qaug, k, vt, ag, hq, hk, hv, hg, bg, ga, gb = _in_proj(
            i, rel_bias, h, norm_gain[i][None, :].astype(F32), w_in, tile(q_norm_gain[i]), tile(k_norm_gain[i]),
            hgrn_lb_logits.astype(F32), hmean)
        ya = _moba(qaug, k, vt, bias_tiles)
        yb = _hgrn(hq, hk, hv, hg)
        h = _merge(i, h, ya, ag, yb, bg, ga, gb, p, hgrn_out_gain[i][None, :].astype(F32),
                   w_up_a, w_up_b, w_out, w_ple, w_ple_gate)
    return h
```

```python
import functools
import math

import jax
import jax.numpy as jnp
from jax import lax
from jax.experimental import pallas as pl
from jax.experimental.pallas import tpu as pltpu

F32 = jnp.float32
BF16 = jnp.bfloat16

MOBA_HEADS = 8
MOBA_HEAD_DIM = 64
MOBA_WIDTH = MOBA_HEADS * MOBA_HEAD_DIM
MOBA_BLOCK = 256
MOBA_TOPK = 3
HGRN_HEADS = 4
HGRN_DIM = 128
HGRN_WIDTH = HGRN_HEADS * HGRN_DIM
EXP_CLIP = 30.0
REL_BUCKETS = 32
REL_MAX_DIST = 2048
RMS_EPS = 1e-6
MASK_VALUE = -1e30

LANES = 128
SUBLANES = 8
VMEM_LIMIT_BYTES = 56 * 1024 * 1024

LOG2E = math.log2(math.e)
HEADS_PER_PAIR = LANES // MOBA_HEAD_DIM
MOBA_PAIRS_PER_STEP = 2
MOBA_HEADS_PER_STEP = HEADS_PER_PAIR * MOBA_PAIRS_PER_STEP
MOBA_GROUP = 2
MASK_LANES = 32
PROJ_CHUNK = 512
PROJ_LOOKAHEAD = 1024
VT_ROWS = MOBA_HEAD_DIM + 16

_MAX_EXACT = REL_BUCKETS // 2
_LAST_BUCKET_DIST = math.ceil(_MAX_EXACT * (REL_MAX_DIST / _MAX_EXACT) ** ((REL_BUCKETS - 1 - _MAX_EXACT) / (REL_BUCKETS - _MAX_EXACT)))
NEAR_BLOCKS = -(-(_LAST_BUCKET_DIST + MOBA_BLOCK - 1) // MOBA_BLOCK)

HGRN_CHUNK = 64
HGRN_LEVELS = (32, 16, 8, 4, 2, 1)


def _sigmoid(x):
    return 1.0 / (1.0 + jnp.exp(-x))


def _silu(x):
    return x * _sigmoid(x)


def _dot(a, b):
    return jnp.dot(a, b, preferred_element_type=F32)


def _dot_nt(a, b):
    return lax.dot_general(a, b, (((1,), (1,)), ((), ())), preferred_element_type=F32)


def _dot_tn(a, b):
    return lax.dot_general(a, b, (((0,), (0,)), ((), ())), preferred_element_type=F32)


def _bias_tiles_kernel(tab_ref, out_ref):
    h = pl.program_id(0)
    d = pl.program_id(1)
    span = 2 * MOBA_BLOCK
    lane = lax.broadcasted_iota(jnp.int32, (SUBLANES, span), 1)
    rel = d * MOBA_BLOCK + jnp.where(lane < MOBA_BLOCK, lane, lane - span)
    n = jnp.maximum(rel, 0)
    nf = jnp.maximum(n, _MAX_EXACT).astype(F32)
    large = _MAX_EXACT + (jnp.log(nf / _MAX_EXACT) / math.log(REL_MAX_DIST / _MAX_EXACT)
                          * (REL_BUCKETS - _MAX_EXACT)).astype(jnp.int32)
    large = jnp.minimum(large, REL_BUCKETS - 1)
    bucket = jnp.where(n < _MAX_EXACT, n, large)
    val = jnp.zeros((SUBLANES, span), F32)
    for b in range(REL_BUCKETS):
        val = jnp.where(bucket == b, tab_ref[b, h], val)
    val = jnp.where(rel >= 0, val * LOG2E, MASK_VALUE)
    val = jnp.where(d < NEAR_BLOCKS, val, 0.0)
    rows = jnp.broadcast_to(val[0:1], (MOBA_BLOCK, span))
    out_ref[0, 0] = pltpu.roll(rows, 0, 1, stride=1, stride_axis=0)[:, :MOBA_BLOCK]


def _bias_tiles(rel_bias):
    return pl.pallas_call(
        _bias_tiles_kernel,
        grid=(MOBA_HEADS, NEAR_BLOCKS + 1),
        in_specs=[pl.BlockSpec(memory_space=pltpu.SMEM)],
        out_specs=pl.BlockSpec((1, 1, MOBA_BLOCK, MOBA_BLOCK), lambda h, d: (h, d, 0, 0)),
        out_shape=jax.ShapeDtypeStruct((MOBA_HEADS, NEAR_BLOCKS + 1, MOBA_BLOCK, MOBA_BLOCK), F32),
        name="t5_bias_tiles",
    )(rel_bias.astype(F32))


def _split_bf16(x):
    hi = x.astype(BF16)
    lo = (x - hi.astype(F32)).astype(BF16)
    return hi, lo


def _moba_queries(blk, qn, kmean, tab_ref, qaug_ref):
    tq = MOBA_BLOCK
    scale = MOBA_HEAD_DIM ** -0.5 * LOG2E
    qt = qn.T
    lane = lax.broadcasted_iota(jnp.int32, (MASK_LANES, LANES), 1)
    kblk = lax.broadcasted_iota(jnp.int32, (MASK_LANES, tq), 0)
    kblk_f = kblk.astype(F32)
    past = kblk < blk
    for pp in range(MOBA_HEADS // HEADS_PER_PAIR):
        qt_pair = qt[pp * LANES:(pp + 1) * LANES]
        q_hi, q_lo = _split_bf16(qt_pair)
        km_pair = kmean[:, pp * LANES:(pp + 1) * LANES]
        for hh in range(HEADS_PER_PAIR):
            h = pp * HEADS_PER_PAIR + hh
            in_head = (lane >= hh * MOBA_HEAD_DIM) & (lane < (hh + 1) * MOBA_HEAD_DIM)
            k_hi, k_lo = _split_bf16(jnp.where(in_head, km_pair, 0.0))
            gate = _dot(k_hi, q_hi) + _dot(k_hi, q_lo) + _dot(k_lo, q_hi)
            gate = jnp.where(past, gate, -jnp.inf)
            sel = jnp.zeros((MASK_LANES, tq), jnp.bool_)
            for _ in range(MOBA_TOPK):
                best = jnp.max(gate, axis=0, keepdims=True)
                first = jnp.min(jnp.where(gate == best, kblk_f, float(MASK_LANES)), axis=0, keepdims=True)
                hit = kblk_f == first
                sel = sel | hit
                gate = jnp.where(hit, -jnp.inf, gate)
            sel = (sel & past) | (kblk == blk)
            far_bias = jnp.where(blk - kblk >= NEAR_BLOCKS, tab_ref[REL_BUCKETS - 1, h] * LOG2E, 0.0)
            mask_hi, mask_lo = _split_bf16(jnp.where(sel, far_bias, MASK_VALUE))
            q_rows = (qt_pair[hh * MOBA_HEAD_DIM:(hh + 1) * MOBA_HEAD_DIM] * scale).astype(BF16)
            parts = [q_rows, mask_hi, mask_lo] if hh == 0 else [mask_hi, mask_lo, q_rows]
            qaug_ref[0, h, 0] = jnp.concatenate(parts, axis=0)


def _in_proj_kernel(layer, tab_ref, x_ref, gain_ref, w_ref, qg_ref, kg_ref, lbl_ref, hmean_ref,
                    qaug_ref, k_ref, vt_ref, ag_ref, hq_ref, hk_ref, hv_ref, hg_ref,
                    bg_ref, ga_ref, gb_ref, kmean_ref, proj_ref):
    blk = pl.program_id(1)
    x = x_ref[0]
    xn = x * lax.rsqrt(jnp.mean(x * x, axis=-1, keepdims=True) + RMS_EPS) * gain_ref[...]
    xn = xn.astype(BF16)

    in_width = w_ref.shape[-1]
    issued = [0]

    def seg(start, width):
        while issued[0] < min(start + width + PROJ_LOOKAHEAD, in_width):
            c0 = issued[0]
            proj_ref[:, c0:c0 + PROJ_CHUNK] = _dot(xn, w_ref[0, :, c0:c0 + PROJ_CHUNK])
            issued[0] = c0 + PROJ_CHUNK
        return proj_ref[:, start:start + width]

    def head_rms(t, gain):
        ms = _dot((t * t).astype(BF16), hmean_ref[...])
        return t * lax.rsqrt(ms + RMS_EPS) * gain

    w = MOBA_WIDTH
    qn = head_rms(seg(0, w), qg_ref[...])
    kn = head_rms(seg(w, w), kg_ref[...])

    @pl.when(blk == 0)
    def _():
        kmean_ref[...] = jnp.zeros_like(kmean_ref)

    _moba_queries(blk, qn, kmean_ref[...], tab_ref, qaug_ref)
    kmean_ref[pl.ds(blk, 1), :] = jnp.mean(kn, axis=0, keepdims=True)
    lane = lax.broadcasted_iota(jnp.int32, (x.shape[0], LANES), 1)
    block_hot = jnp.where((lane & (MASK_LANES - 1)) == blk, 1.0, 0.0)
    for pp in range(MOBA_HEADS // HEADS_PER_PAIR):
        kp = kn[:, pp * LANES:(pp + 1) * LANES]
        k_ref[0, HEADS_PER_PAIR * pp] = jnp.where(lane < MOBA_HEAD_DIM, kp, block_hot).astype(BF16)
        k_ref[0, HEADS_PER_PAIR * pp + 1] = jnp.where(lane >= MOBA_HEAD_DIM, kp, block_hot).astype(BF16)
    vt = seg(2 * w, w).T
    extra = lax.broadcasted_iota(jnp.int32, (VT_ROWS - MOBA_HEAD_DIM, x.shape[0]), 0)
    for h in range(MOBA_HEADS):
        vt_ref[0, h, :MOBA_HEAD_DIM] = vt[h * MOBA_HEAD_DIM:(h + 1) * MOBA_HEAD_DIM].astype(BF16)
        vt_ref[0, h, MOBA_HEAD_DIM:] = jnp.where(extra == 0, 1.0, 0.0).astype(BF16)
    ag_ref[0] = _silu(seg(3 * w, w)).astype(BF16)

    base = 4 * w
    hw = HGRN_WIDTH
    hq_ref[0] = _silu(seg(base, hw))
    lbl = lbl_ref[...]
    e = jnp.exp(lbl - jnp.max(lbl, axis=0, keepdims=True))
    sm = e / jnp.sum(e, axis=0, keepdims=True)
    lb = jnp.zeros((1, hw), F32)
    for j in range(1, layer + 1):
        lb = lb + sm[j:j + 1, :]
    z = seg(base + hw, hw)
    e = jnp.exp(-jnp.abs(z))
    pos = z >= 0.0
    inv = 1.0 / (1.0 + e)
    clipped = jnp.where(pos, e, jnp.minimum(1.0 / e, math.exp(EXP_CLIP)))
    hg_ref[0] = jnp.minimum(z, 0.0) * LOG2E + jnp.log2((1.0 + lb * clipped) * inv)
    hk_ref[0] = (1.0 - lb) * jnp.where(pos, e, 1.0) * inv
    hv_ref[0] = seg(base + 2 * hw, hw)
    bg_ref[0] = _silu(seg(base + 3 * hw, hw)).astype(BF16)

    base = base + 4 * hw
    d = x.shape[-1]
    ga_ref[0] = _sigmoid(seg(base, d)).astype(BF16)
    gb_ref[0] = _sigmoid(seg(base + d, d)).astype(BF16)


def _in_proj(layer, rel_bias, h, norm_gain, w_in, q_gain, k_gain, lb_logits, hmean):
    b, s, d = h.shape
    depth = w_in.shape[0]
    tm = MOBA_BLOCK
    nt = s // tm
    assert nt <= MASK_LANES and HEADS_PER_PAIR == 2
    in_width = w_in.shape[-1]
    tok = lambda width: pl.BlockSpec((1, tm, width), lambda bi, ti: (bi, ti, 0))
    const2 = lambda shape: pl.BlockSpec(shape, lambda bi, ti: (0, 0))
    tok_shape = lambda width, dtype: jax.ShapeDtypeStruct((b, s, width), dtype)
    outs = (
        (pl.BlockSpec((1, MOBA_HEADS, 1, LANES, tm), lambda bi, ti: (bi, 0, ti, 0, 0)),
         jax.ShapeDtypeStruct((b, MOBA_HEADS, nt, LANES, tm), BF16)),
        (pl.BlockSpec((1, MOBA_HEADS, tm, LANES), lambda bi, ti: (bi, 0, ti, 0)),
         jax.ShapeDtypeStruct((b, MOBA_HEADS, s, LANES), BF16)),
        (pl.BlockSpec((1, MOBA_HEADS, VT_ROWS, tm), lambda bi, ti: (bi, 0, 0, ti)),
         jax.ShapeDtypeStruct((b, MOBA_HEADS, VT_ROWS, s), BF16)),
        (tok(MOBA_WIDTH), tok_shape(MOBA_WIDTH, BF16)),
        (tok(HGRN_WIDTH), tok_shape(HGRN_WIDTH, F32)),
        (tok(HGRN_WIDTH), tok_shape(HGRN_WIDTH, F32)),
        (tok(HGRN_WIDTH), tok_shape(HGRN_WIDTH, F32)),
        (tok(HGRN_WIDTH), tok_shape(HGRN_WIDTH, F32)),
        (tok(HGRN_WIDTH), tok_shape(HGRN_WIDTH, BF16)),
        (tok(d), tok_shape(d, BF16)),
        (tok(d), tok_shape(d, BF16)),
    )
    return pl.pallas_call(
        functools.partial(_in_proj_kernel, layer),
        grid=(b, nt),
        in_specs=[
            pl.BlockSpec(memory_space=pltpu.SMEM),
            tok(d),
            const2((1, d)),
            pl.BlockSpec((1, d, in_width), lambda bi, ti: (layer, 0, 0), pipeline_mode=pl.Buffered(1)),
            const2((1, MOBA_WIDTH)),
            const2((1, MOBA_WIDTH)),
            const2((depth, HGRN_WIDTH)),
            const2((MOBA_WIDTH, MOBA_WIDTH)),
        ],
        out_specs=[o[0] for o in outs],
        out_shape=[o[1] for o in outs],
        scratch_shapes=[pltpu.VMEM((MASK_LANES, MOBA_WIDTH), F32), pltpu.VMEM((tm, in_width), F32)],
        compiler_params=pltpu.CompilerParams(
            dimension_semantics=("arbitrary", "arbitrary"), vmem_limit_bytes=VMEM_LIMIT_BYTES),
        name=f"in_proj_l{layer}",
    )(rel_bias.astype(F32), h, norm_gain, w_in, q_gain, k_gain, lb_logits, hmean)


def _moba_kernel(qaug_ref, qnext_ref, k_ref, vt_ref, bias_ref, out_ref, s0_ref, s1_ref, max0_ref, max1_ref):
    blk = pl.program_id(2)
    tq = MOBA_BLOCK
    group_rows = MOBA_GROUP * MOBA_BLOCK

    def score_group(g, bufs, near, q_ref=qaug_ref, q_blk=blk):
        s_ref, max_ref = bufs
        rows = pl.ds(pl.multiple_of(g * group_rows, group_rows), group_rows)
        for h in range(MOBA_HEADS_PER_STEP):
            s = _dot(k_ref[0, h, rows, :], q_ref[0, h, 0])
            if near:
                parts = []
                for u in range(MOBA_GROUP):
                    dist = q_blk - (g * MOBA_GROUP + u)
                    tile = bias_ref[h, jnp.where((dist >= 0) & (dist < NEAR_BLOCKS), dist, NEAR_BLOCKS)]
                    parts.append(s[u * MOBA_BLOCK:(u + 1) * MOBA_BLOCK] + tile)
                s = jnp.concatenate(parts, axis=0)
            s_ref[h] = s
            max_ref[h] = jnp.max(s.reshape(group_rows // SUBLANES, SUBLANES, tq), axis=0)

    def group(g, bufs, state):
        s_ref, max_ref = bufs
        rows = pl.ds(pl.multiple_of(g * group_rows, group_rows), group_rows)
        new = []
        for h in range(MOBA_HEADS_PER_STEP):
            m, acc = state[2 * h:2 * h + 2]
            m_new = jnp.maximum(m, jnp.max(max_ref[h], axis=0, keepdims=True))
            p = jnp.exp2(s_ref[h] - m_new).astype(BF16)
            acc = jnp.exp2(m - m_new) * acc + _dot(vt_ref[0, h, :, rows], p)
            new += [m_new, acc]
        return new

    state = []
    for _ in range(MOBA_HEADS_PER_STEP):
        state += [jnp.full((1, tq), MASK_VALUE, F32), jnp.zeros((VT_ROWS, tq), F32)]
    buf0, buf1 = (s0_ref, max0_ref), (s1_ref, max1_ref)

    def group_pair(i, state, near):
        score_group(2 * i + 1, buf1, near)
        state = group(2 * i, buf0, state)
        score_group(2 * i + 2, buf0, near)
        return group(2 * i + 1, buf1, state)

    pair_blocks = 2 * MOBA_GROUP
    last = blk // pair_blocks
    n_far = jnp.maximum(jnp.maximum(blk - (NEAR_BLOCKS - 1), 0) // pair_blocks - 1, 0)

    @pl.when(blk == 0)
    def _():
        score_group(0, buf0, True)

    state = lax.fori_loop(0, n_far, lambda i, st: group_pair(i, st, False), state)
    state = lax.fori_loop(n_far, last, lambda i, st: group_pair(i, st, True), state)

    second = (blk % pair_blocks) >= MOBA_GROUP
    score_group(2 * last + 1, buf1, True)
    state = group(2 * last, buf0, state)
    score_group(0, buf0, True, qnext_ref, blk + 1)
    state = lax.cond(second, lambda st: tuple(group(2 * last + 1, buf1, list(st))),
                     lambda st: tuple(st), tuple(state))

    accs = state[1::2]
    out_t = jnp.concatenate([a[:MOBA_HEAD_DIM] / a[MOBA_HEAD_DIM:MOBA_HEAD_DIM + 1] for a in accs], axis=0)
    out_ref[0] = out_t.T.astype(out_ref.dtype)


def _moba(qaug, k, vt, bias_tiles):
    b, _, s, _ = k.shape
    nb = s // MOBA_BLOCK
    assert nb % (2 * MOBA_GROUP) == 0
    hs = MOBA_HEADS_PER_STEP
    width = hs * MOBA_HEAD_DIM
    return pl.pallas_call(
        _moba_kernel,
        grid=(b, MOBA_HEADS // hs, nb),
        in_specs=[
            pl.BlockSpec((1, hs, 1, LANES, MOBA_BLOCK), lambda bi, pi, ti: (bi, pi, ti, 0, 0)),
            pl.BlockSpec((1, hs, 1, LANES, MOBA_BLOCK),
                         lambda bi, pi, ti: (bi, pi, jnp.minimum(ti + 1, nb - 1), 0, 0)),
            pl.BlockSpec((1, hs, s, LANES), lambda bi, pi, ti: (bi, pi, 0, 0)),
            pl.BlockSpec((1, hs, VT_ROWS, s), lambda bi, pi, ti: (bi, pi, 0, 0)),
            pl.BlockSpec((hs, NEAR_BLOCKS + 1, MOBA_BLOCK, MOBA_BLOCK),
                         lambda bi, pi, ti: (pi, 0, 0, 0), pipeline_mode=pl.Buffered(1)),
        ],
        out_specs=pl.BlockSpec((1, MOBA_BLOCK, width), lambda bi, pi, ti: (bi, ti, pi)),
        out_shape=jax.ShapeDtypeStruct((b, s, MOBA_WIDTH), BF16),
        scratch_shapes=[pltpu.VMEM((hs, MOBA_GROUP * MOBA_BLOCK, MOBA_BLOCK), F32),
                        pltpu.VMEM((hs, MOBA_GROUP * MOBA_BLOCK, MOBA_BLOCK), F32),
                        pltpu.VMEM((hs, SUBLANES, MOBA_BLOCK), F32),
                        pltpu.VMEM((hs, SUBLANES, MOBA_BLOCK), F32)],
        compiler_params=pltpu.CompilerParams(
            dimension_semantics=("arbitrary", "arbitrary", "arbitrary"), vmem_limit_bytes=VMEM_LIMIT_BYTES),
        name="moba_attention",
    )(qaug, qaug, k, vt, bias_tiles)


def _chunk_cumsum(g):
    groups = HGRN_CHUNK // SUBLANES
    g3 = g.reshape(groups, SUBLANES, g.shape[-1])
    sub = lax.broadcasted_iota(jnp.int32, g3.shape, 1)
    shift = 1
    while shift < SUBLANES:
        g3 = g3 + jnp.where(sub >= shift, pltpu.roll(g3, shift, axis=1), 0.0)
        shift *= 2
    rows = []
    run = None
    for i in range(groups):
        cur = g3[i] if run is None else g3[i] + run
        rows.append(cur)
        run = cur[SUBLANES - 1:SUBLANES, :]
    return jnp.concatenate(rows, axis=0)


def _hgrn_masks():
    c = HGRN_CHUNK
    row = lax.broadcasted_iota(jnp.int32, (c, 1), 0)
    rt = lax.broadcasted_iota(jnp.int32, (c, c), 0)
    cs = lax.broadcasted_iota(jnp.int32, (c, c), 1)
    levels = []
    for half in HGRN_LEVELS:
        shift = int(math.log2(2 * half))
        valid = ((rt >> shift) == (cs >> shift)) & ((rt & half) != 0) & ((cs & half) == 0)
        levels.append(((row & half) != 0, valid))
    return levels, rt == cs


def _hgrn_chunk_matmuls(q, k, v, g2, state_t, level_masks):
    c = HGRN_CHUNK
    cum = _chunk_cumsum(g2)
    last = cum[c - 1:c, :]
    v16 = v.astype(BF16)

    o = _dot_nt((q * jnp.exp2(cum)).astype(BF16), state_t.astype(BF16))
    k_end = (k * jnp.exp2(last - cum)).astype(BF16)
    new_state = state_t * jnp.exp2(last) + _dot_tn(v16, k_end)

    scores = jnp.zeros((c, c), F32)
    groups = c // SUBLANES
    width = cum.shape[-1]
    cum3 = cum.reshape(groups, SUBLANES, width)
    sub = lax.broadcasted_iota(jnp.int32, (groups, SUBLANES, width), 1)

    def row_of_group(r):
        return jnp.broadcast_to(cum3[:, r:r + 1, :], cum3.shape)

    for half, (is_q, valid) in zip(HGRN_LEVELS, level_masks):
        if half == 1:
            exponent = jnp.where(is_q, g2, 0.0)
        else:
            if half >= SUBLANES:
                ref = jnp.concatenate(
                    [jnp.broadcast_to(cum[lo + half - 1:lo + half, :], (2 * half, width))
                     for lo in range(0, c, 2 * half)], axis=0)
            elif half == 4:
                ref = row_of_group(3).reshape(c, width)
            else:
                ref = jnp.where(sub < 4, row_of_group(1), row_of_group(5)).reshape(c, width)
            exponent = jnp.where(is_q, cum - ref, ref - cum)
        x = (jnp.where(is_q, q, k) * jnp.exp2(exponent)).astype(BF16)
        scores = scores + jnp.where(valid, _dot_nt(x, x), 0.0)
    return o, new_state, scores, v16


def _hgrn_kernel(q_ref, k_ref, v_ref, g_ref, out_ref, state_ref):
    @pl.when(pl.program_id(1) == 0)
    def _():
        state_ref[...] = jnp.zeros_like(state_ref)

    n_chunks = q_ref.shape[1] // HGRN_CHUNK
    masks = _hgrn_masks()

    level_masks, on_diagonal = masks

    def chunk(ci, carry):
        rows = pl.ds(pl.multiple_of(ci * HGRN_CHUNK, HGRN_CHUNK), HGRN_CHUNK)
        partial = []
        for h in range(HGRN_HEADS):
            cols = slice(h * HGRN_DIM, (h + 1) * HGRN_DIM)
            q, k = q_ref[0, rows, cols], k_ref[0, rows, cols]
            o, new_state, scores, v16 = _hgrn_chunk_matmuls(
                q, k, v_ref[0, rows, cols], g_ref[0, rows, cols], state_ref[h], level_masks)
            state_ref[h] = new_state
            partial.append((jnp.sum(q * k, axis=-1, keepdims=True), o, scores, v16))
        for h, (self_score, o, scores, v16) in enumerate(partial):
            scores = jnp.where(on_diagonal, self_score, scores)
            out = o + _dot(scores.astype(BF16), v16)
            out_ref[0, rows, h * HGRN_DIM:(h + 1) * HGRN_DIM] = out.astype(out_ref.dtype)
        return carry

    lax.fori_loop(0, n_chunks, chunk, 0, unroll=8)


def _hgrn(hq, hk, hv, hg):
    b, s, w = hq.shape
    ts = min(s, 512)
    spec = pl.BlockSpec((1, ts, w), lambda bi, ti: (bi, ti, 0))
    return pl.pallas_call(
        _hgrn_kernel,
        grid=(b, s // ts),
        in_specs=[spec, spec, spec, spec],
        out_specs=spec,
        out_shape=jax.ShapeDtypeStruct((b, s, w), BF16),
        scratch_shapes=[pltpu.VMEM((HGRN_HEADS, HGRN_DIM, HGRN_DIM), F32)],
        compiler_params=pltpu.CompilerParams(
            dimension_semantics=("arbitrary", "arbitrary"), vmem_limit_bytes=VMEM_LIMIT_BYTES),
        name="hgrn2_recurrence",
    )(hq, hk, hv, hg)


def _merge_kernel(h_ref, ya_ref, ag_ref, yb_ref, bg_ref, ga_ref, gb_ref, p_ref, og_ref,
                  wa_ref, wb_ref, wo_ref, wp_ref, wg_ref, out_ref):
    ya = _dot(ya_ref[0] * ag_ref[0], wa_ref[0])

    yb = yb_ref[0].astype(F32)
    normed = []
    for h in range(HGRN_HEADS):
        t = yb[:, h * HGRN_DIM:(h + 1) * HGRN_DIM]
        normed.append(t * lax.rsqrt(jnp.mean(t * t, axis=-1, keepdims=True) + RMS_EPS))
    yb = jnp.concatenate(normed, axis=-1) * og_ref[...]
    yb = _dot((yb * bg_ref[0]).astype(BF16), wb_ref[0])

    merged = ga_ref[0] * ya + gb_ref[0] * yb
    h = h_ref[0] + _dot(merged.astype(BF16), wo_ref[0])
    ple = _dot(p_ref[0, 0].astype(BF16), wp_ref[0]) * _sigmoid(_dot(h.astype(BF16), wg_ref[0]))
    out_ref[0] = h + ple


def _merge(layer, h, ya, ag, yb, bg, ga, gb, p, out_gain, w_up_a, w_up_b, w_out, w_ple, w_ple_gate):
    b, s, d = h.shape
    tm = min(s, 512)
    tok = lambda width: pl.BlockSpec((1, tm, width), lambda bi, ti: (bi, ti, 0))
    wspec = lambda w: pl.BlockSpec((1,) + w.shape[1:], lambda bi, ti: (layer, 0, 0), pipeline_mode=pl.Buffered(1))
    return pl.pallas_call(
        _merge_kernel,
        grid=(b, s // tm),
        in_specs=[
            tok(d), tok(MOBA_WIDTH), tok(MOBA_WIDTH), tok(HGRN_WIDTH), tok(HGRN_WIDTH), tok(d), tok(d),
            pl.BlockSpec((1, 1, tm, p.shape[-1]), lambda bi, ti: (layer, bi, ti, 0)),
            pl.BlockSpec((1, HGRN_WIDTH), lambda bi, ti: (0, 0)),
            wspec(w_up_a), wspec(w_up_b), wspec(w_out), wspec(w_ple), wspec(w_ple_gate),
        ],
        out_specs=tok(d),
        out_shape=jax.ShapeDtypeStruct((b, s, d), F32),
        compiler_params=pltpu.CompilerParams(
            dimension_semantics=("arbitrary", "arbitrary"), vmem_limit_bytes=VMEM_LIMIT_BYTES),
        name=f"merge_l{layer}",
    )(h, ya, ag, yb, bg, ga, gb, p, out_gain, w_up_a, w_up_b, w_out, w_ple, w_ple_gate)


def kernel(x, p, norm_gain, w_in, q_norm_gain, k_norm_gain, rel_bias, hgrn_lb_logits, hgrn_out_gain,
           w_up_a, w_up_b, w_out, w_ple, w_ple_gate):
    b, s, d = x.shape
    depth = w_in.shape[0]
    assert s % MOBA_BLOCK == 0 and s % HGRN_CHUNK == 0

    w_in, w_up_a, w_up_b, w_out, w_ple, w_ple_gate = (
        w.astype(BF16) for w in (w_in, w_up_a, w_up_b, w_out, w_ple, w_ple_gate))
    head_id = jnp.arange(MOBA_WIDTH) // MOBA_HEAD_DIM
    hmean = ((head_id[:, None] == head_id[None, :]).astype(F32) / MOBA_HEAD_DIM).astype(BF16)
    tile = lambda g: jnp.tile(g.astype(F32), MOBA_HEADS)[None, :]

    bias_tiles = _bias_tiles(rel_bias)
    h = x
    for i in range(depth):
        qaug, k, vt, ag, hq, hk, hv, hg, bg, ga, gb = _in_proj(
            i, rel_bias, h, norm_gain[i][None, :].astype(F32), w_in, tile(q_norm_gain[i]), tile(k_norm_gain[i]),
            hgrn_lb_logits.astype(F32), hmean)
        ya = _moba(qaug, k, vt, bias_tiles)
        yb = _hgrn(hq, hk, hv, hg)
        h = _merge(i, h, ya, ag, yb, bg, ga, gb, p, hgrn_out_gain[i][None, :].astype(F32),
                   w_up_a, w_up_b, w_out, w_ple, w_ple_gate)
    return h
```

```python
import functools
import math

import jax
import jax.numpy as jnp
from jax import lax
from jax.experimental import pallas as pl
from jax.experimental.pallas import tpu as pltpu

F32 = jnp.float32
BF16 = jnp.bfloat16

MOBA_HEADS = 8
MOBA_HEAD_DIM = 64
MOBA_WIDTH = MOBA_HEADS * MOBA_HEAD_DIM
MOBA_BLOCK = 256
MOBA_TOPK = 3
HGRN_HEADS = 4
HGRN_DIM = 128
HGRN_WIDTH = HGRN_HEADS * HGRN_DIM
EXP_CLIP = 30.0
REL_BUCKETS = 32
REL_MAX_DIST = 2048
RMS_EPS = 1e-6
MASK_VALUE = -1e30

LANES = 128
SUBLANES = 8
VMEM_LIMIT_BYTES = 56 * 1024 * 1024

LOG2E = math.log2(math.e)
HEADS_PER_PAIR = LANES // MOBA_HEAD_DIM
MOBA_PAIRS_PER_STEP = 2
MOBA_HEADS_PER_STEP = HEADS_PER_PAIR * MOBA_PAIRS_PER_STEP
MOBA_GROUP = 2
MASK_LANES = 32
IN_PROJ_TILE = 512
VT_ROWS = MOBA_HEAD_DIM + 16

_MAX_EXACT = REL_BUCKETS // 2
_LAST_BUCKET_DIST = math.ceil(_MAX_EXACT * (REL_MAX_DIST / _MAX_EXACT) ** ((REL_BUCKETS - 1 - _MAX_EXACT) / (REL_BUCKETS - _MAX_EXACT)))
NEAR_BLOCKS = -(-(_LAST_BUCKET_DIST + MOBA_BLOCK - 1) // MOBA_BLOCK)

HGRN_CHUNK = 64
HGRN_LEVELS = (32, 16, 8, 4, 2, 1)


def _sigmoid(x):
    return 0.5 * jnp.tanh(0.5 * x) + 0.5


def _silu(x):
    return x * _sigmoid(x)


def _dot(a, b):
    return jnp.dot(a, b, preferred_element_type=F32)


def _dot_nt(a, b):
    return lax.dot_general(a, b, (((1,), (1,)), ((), ())), preferred_element_type=F32)


def _dot_tn(a, b):
    return lax.dot_general(a, b, (((0,), (0,)), ((), ())), preferred_element_type=F32)


def _bias_tiles_kernel(tab_ref, out_ref):
    h = pl.program_id(0)
    d = pl.program_id(1)
    span = 2 * MOBA_BLOCK
    lane = lax.broadcasted_iota(jnp.int32, (SUBLANES, span), 1)
    rel = d * MOBA_BLOCK + jnp.where(lane < MOBA_BLOCK, lane, lane - span)
    n = jnp.maximum(rel, 0)
    nf = jnp.maximum(n, _MAX_EXACT).astype(F32)
    large = _MAX_EXACT + (jnp.log(nf / _MAX_EXACT) / math.log(REL_MAX_DIST / _MAX_EXACT)
                          * (REL_BUCKETS - _MAX_EXACT)).astype(jnp.int32)
    large = jnp.minimum(large, REL_BUCKETS - 1)
    bucket = jnp.where(n < _MAX_EXACT, n, large)
    val = jnp.zeros((SUBLANES, span), F32)
    for b in range(REL_BUCKETS):
        val = jnp.where(bucket == b, tab_ref[b, h], val)
    val = jnp.where(rel >= 0, val * LOG2E, MASK_VALUE)
    val = jnp.where(d < NEAR_BLOCKS, val, 0.0)
    rows = jnp.broadcast_to(val[0:1], (MOBA_BLOCK, span))
    out_ref[0, 0] = pltpu.roll(rows, 0, 1, stride=1, stride_axis=0)[:, :MOBA_BLOCK]


def _bias_tiles(rel_bias):
    return pl.pallas_call(
        _bias_tiles_kernel,
        grid=(MOBA_HEADS, NEAR_BLOCKS + 1),
        in_specs=[pl.BlockSpec(memory_space=pltpu.SMEM)],
        out_specs=pl.BlockSpec((1, 1, MOBA_BLOCK, MOBA_BLOCK), lambda h, d: (h, d, 0, 0)),
        out_shape=jax.ShapeDtypeStruct((MOBA_HEADS, NEAR_BLOCKS + 1, MOBA_BLOCK, MOBA_BLOCK), F32),
        name="t5_bias_tiles",
    )(rel_bias.astype(F32))


def _split_bf16(x):
    hi = x.astype(BF16)
    lo = (x - hi.astype(F32)).astype(BF16)
    return hi, lo


def _moba_queries(blk, slot, qn, kmean, tab_ref, qaug_ref):
    tq = MOBA_BLOCK
    scale = MOBA_HEAD_DIM ** -0.5 * LOG2E
    qt = qn.T
    lane = lax.broadcasted_iota(jnp.int32, (MASK_LANES, LANES), 1)
    kblk = lax.broadcasted_iota(jnp.int32, (MASK_LANES, tq), 0)
    kblk_f = kblk.astype(F32)
    past = kblk < blk
    for pp in range(MOBA_HEADS // HEADS_PER_PAIR):
        qt_pair = qt[pp * LANES:(pp + 1) * LANES]
        q_hi, q_lo = _split_bf16(qt_pair)
        km_pair = kmean[:, pp * LANES:(pp + 1) * LANES]
        for hh in range(HEADS_PER_PAIR):
            h = pp * HEADS_PER_PAIR + hh
            in_head = (lane >= hh * MOBA_HEAD_DIM) & (lane < (hh + 1) * MOBA_HEAD_DIM)
            k_hi, k_lo = _split_bf16(jnp.where(in_head, km_pair, 0.0))
            gate = _dot(k_hi, q_hi) + _dot(k_hi, q_lo) + _dot(k_lo, q_hi)
            gate = jnp.where(past, gate, -jnp.inf)
            sel = jnp.zeros((MASK_LANES, tq), jnp.bool_)
            for _ in range(MOBA_TOPK):
                best = jnp.max(gate, axis=0, keepdims=True)
                first = jnp.min(jnp.where(gate == best, kblk_f, float(MASK_LANES)), axis=0, keepdims=True)
                hit = kblk_f == first
                sel = sel | hit
                gate = jnp.where(hit, -jnp.inf, gate)
            sel = (sel & past) | (kblk == blk)
            far_bias = jnp.where(blk - kblk >= NEAR_BLOCKS, tab_ref[REL_BUCKETS - 1, h] * LOG2E, 0.0)
            mask_hi, mask_lo = _split_bf16(jnp.where(sel, far_bias, MASK_VALUE))
            q_rows = (qt_pair[hh * MOBA_HEAD_DIM:(hh + 1) * MOBA_HEAD_DIM] * scale).astype(BF16)
            parts = [q_rows, mask_hi, mask_lo] if hh == 0 else [mask_hi, mask_lo, q_rows]
            qaug_ref[0, h, slot] = jnp.concatenate(parts, axis=0)


def _in_proj_kernel(layer, tab_ref, x_ref, gain_ref, w_ref, qg_ref, kg_ref, lbl_ref, hmean_ref,
                    qaug_ref, k_ref, vt_ref, ag_ref, hq_ref, hk_ref, hv_ref, hg_ref,
                    bg_ref, ga_ref, gb_ref, kmean_ref):
    tile_blocks = x_ref.shape[1] // MOBA_BLOCK
    first_blk = pl.program_id(1) * tile_blocks
    x = x_ref[0]
    xn = x * lax.rsqrt(jnp.mean(x * x, axis=-1, keepdims=True) + RMS_EPS) * gain_ref[...]
    xn = xn.astype(BF16)

    def seg(start, width):
        return _dot(xn, w_ref[0, :, start:start + width])

    def head_rms(t, gain):
        ms = _dot((t * t).astype(BF16), hmean_ref[...])
        return t * lax.rsqrt(ms + RMS_EPS) * gain

    w = MOBA_WIDTH
    qn = head_rms(seg(0, w), qg_ref[...])
    kn = head_rms(seg(w, w), kg_ref[...])

    @pl.when(first_blk == 0)
    def _():
        kmean_ref[...] = jnp.zeros_like(kmean_ref)

    for j in range(tile_blocks):
        rows = slice(j * MOBA_BLOCK, (j + 1) * MOBA_BLOCK)
        _moba_queries(first_blk + j, j, qn[rows], kmean_ref[...], tab_ref, qaug_ref)
        kmean_ref[pl.ds(first_blk + j, 1), :] = jnp.mean(kn[rows], axis=0, keepdims=True)
    lane = lax.broadcasted_iota(jnp.int32, (x.shape[0], LANES), 1)
    row = lax.broadcasted_iota(jnp.int32, (x.shape[0], LANES), 0)
    row_blk = first_blk + jnp.right_shift(row, MOBA_BLOCK.bit_length() - 1)
    block_hot = jnp.where((lane & (MASK_LANES - 1)) == row_blk, 1.0, 0.0)
    for pp in range(MOBA_HEADS // HEADS_PER_PAIR):
        kp = kn[:, pp * LANES:(pp + 1) * LANES]
        k_ref[0, HEADS_PER_PAIR * pp] = jnp.where(lane < MOBA_HEAD_DIM, kp, block_hot).astype(BF16)
        k_ref[0, HEADS_PER_PAIR * pp + 1] = jnp.where(lane >= MOBA_HEAD_DIM, kp, block_hot).astype(BF16)
    vt = seg(2 * w, w).T
    extra = lax.broadcasted_iota(jnp.int32, (VT_ROWS - MOBA_HEAD_DIM, x.shape[0]), 0)
    for h in range(MOBA_HEADS):
        vt_ref[0, h, :MOBA_HEAD_DIM] = vt[h * MOBA_HEAD_DIM:(h + 1) * MOBA_HEAD_DIM].astype(BF16)
        vt_ref[0, h, MOBA_HEAD_DIM:] = jnp.where(extra == 0, 1.0, 0.0).astype(BF16)
    ag_ref[0] = _silu(seg(3 * w, w)).astype(BF16)

    base = 4 * w
    hw = HGRN_WIDTH
    hq_ref[0] = _silu(seg(base, hw))
    lbl = lbl_ref[...]
    e = jnp.exp(lbl - jnp.max(lbl, axis=0, keepdims=True))
    sm = e / jnp.sum(e, axis=0, keepdims=True)
    lb = jnp.zeros((1, hw), F32)
    for j in range(1, layer + 1):
        lb = lb + sm[j:j + 1, :]
    z = seg(base + hw, hw)
    e = jnp.exp(-jnp.abs(z))
    pos = z >= 0.0
    inv = 1.0 / (1.0 + e)
    clipped = jnp.where(pos, e, jnp.minimum(1.0 / e, math.exp(EXP_CLIP)))
    hg_ref[0] = jnp.minimum(z, 0.0) * LOG2E + jnp.log2((1.0 + lb * clipped) * inv)
    hk_ref[0] = (1.0 - lb) * jnp.where(pos, e, 1.0) * inv
    hv_ref[0] = seg(base + 2 * hw, hw)
    bg_ref[0] = _silu(seg(base + 3 * hw, hw)).astype(BF16)

    base = base + 4 * hw
    d = x.shape[-1]
    ga_ref[0] = _sigmoid(seg(base, d)).astype(BF16)
    gb_ref[0] = _sigmoid(seg(base + d, d)).astype(BF16)


def _in_proj(layer, rel_bias, h, norm_gain, w_in, q_gain, k_gain, lb_logits, hmean):
    b, s, d = h.shape
    depth = w_in.shape[0]
    tm = IN_PROJ_TILE
    nt = s // tm
    tile_blocks = tm // MOBA_BLOCK
    nb = s // MOBA_BLOCK
    assert s % tm == 0 and nb <= MASK_LANES and HEADS_PER_PAIR == 2
    in_width = w_in.shape[-1]
    tok = lambda width: pl.BlockSpec((1, tm, width), lambda bi, ti: (bi, ti, 0))
    const2 = lambda shape: pl.BlockSpec(shape, lambda bi, ti: (0, 0))
    tok_shape = lambda width, dtype: jax.ShapeDtypeStruct((b, s, width), dtype)
    outs = (
        (pl.BlockSpec((1, MOBA_HEADS, tile_blocks, LANES, MOBA_BLOCK), lambda bi, ti: (bi, 0, ti, 0, 0)),
         jax.ShapeDtypeStruct((b, MOBA_HEADS, nb, LANES, MOBA_BLOCK), BF16)),
        (pl.BlockSpec((1, MOBA_HEADS, tm, LANES), lambda bi, ti: (bi, 0, ti, 0)),
         jax.ShapeDtypeStruct((b, MOBA_HEADS, s, LANES), BF16)),
        (pl.BlockSpec((1, MOBA_HEADS, VT_ROWS, tm), lambda bi, ti: (bi, 0, 0, ti)),
         jax.ShapeDtypeStruct((b, MOBA_HEADS, VT_ROWS, s), BF16)),
        (tok(MOBA_WIDTH), tok_shape(MOBA_WIDTH, BF16)),
        (tok(HGRN_WIDTH), tok_shape(HGRN_WIDTH, F32)),
        (tok(HGRN_WIDTH), tok_shape(HGRN_WIDTH, F32)),
        (tok(HGRN_WIDTH), tok_shape(HGRN_WIDTH, F32)),
        (tok(HGRN_WIDTH), tok_shape(HGRN_WIDTH, F32)),
        (tok(HGRN_WIDTH), tok_shape(HGRN_WIDTH, BF16)),
        (tok(d), tok_shape(d, BF16)),
        (tok(d), tok_shape(d, BF16)),
    )
    return pl.pallas_call(
        functools.partial(_in_proj_kernel, layer),
        grid=(b, nt),
        in_specs=[
            pl.BlockSpec(memory_space=pltpu.SMEM),
            tok(d),
            const2((1, d)),
            pl.BlockSpec((1, d, in_width), lambda bi, ti: (layer, 0, 0), pipeline_mode=pl.Buffered(1)),
            const2((1, MOBA_WIDTH)),
            const2((1, MOBA_WIDTH)),
            const2((depth, HGRN_WIDTH)),
            const2((MOBA_WIDTH, MOBA_WIDTH)),
        ],
        out_specs=[o[0] for o in outs],
        out_shape=[o[1] for o in outs],
        scratch_shapes=[pltpu.VMEM((MASK_LANES, MOBA_WIDTH), F32)],
        compiler_params=pltpu.CompilerParams(
            dimension_semantics=("arbitrary", "arbitrary"), vmem_limit_bytes=VMEM_LIMIT_BYTES),
        name=f"in_proj_l{layer}",
    )(rel_bias.astype(F32), h, norm_gain, w_in, q_gain, k_gain, lb_logits, hmean)


def _moba_kernel(qaug_ref, qnext_ref, k_ref, vt_ref, bias_ref, out_ref, s0_ref, s1_ref, max0_ref, max1_ref):
    blk = pl.program_id(2)
    tq = MOBA_BLOCK
    group_rows = MOBA_GROUP * MOBA_BLOCK

    def score_group(g, bufs, near, q_ref=qaug_ref, q_blk=blk):
        s_ref, max_ref = bufs
        rows = pl.ds(pl.multiple_of(g * group_rows, group_rows), group_rows)
        for h in range(MOBA_HEADS_PER_STEP):
            s = _dot(k_ref[0, h, rows, :], q_ref[0, h, 0])
            if near:
                parts = []
                for u in range(MOBA_GROUP):
                    dist = q_blk - (g * MOBA_GROUP + u)
                    tile = bias_ref[h, jnp.where((dist >= 0) & (dist < NEAR_BLOCKS), dist, NEAR_BLOCKS)]
                    parts.append(s[u * MOBA_BLOCK:(u + 1) * MOBA_BLOCK] + tile)
                s = jnp.concatenate(parts, axis=0)
            s_ref[h] = s
            max_ref[h] = jnp.max(s.reshape(group_rows // SUBLANES, SUBLANES, tq), axis=0)

    def group(g, bufs, state):
        s_ref, max_ref = bufs
        rows = pl.ds(pl.multiple_of(g * group_rows, group_rows), group_rows)
        new = []
        for h in range(MOBA_HEADS_PER_STEP):
            m, acc = state[2 * h:2 * h + 2]
            m_new = jnp.maximum(m, jnp.max(max_ref[h], axis=0, keepdims=True))
            p = jnp.exp2(s_ref[h] - m_new).astype(BF16)
            acc = jnp.exp2(m - m_new) * acc + _dot(vt_ref[0, h, :, rows], p)
            new += [m_new, acc]
        return new

    state = []
    for _ in range(MOBA_HEADS_PER_STEP):
        state += [jnp.full((1, tq), MASK_VALUE, F32), jnp.zeros((VT_ROWS, tq), F32)]
    buf0, buf1 = (s0_ref, max0_ref), (s1_ref, max1_ref)

    def group_pair(i, state, near):
        score_group(2 * i + 1, buf1, near)
        state = group(2 * i, buf0, state)
        score_group(2 * i + 2, buf0, near)
        return group(2 * i + 1, buf1, state)

    pair_blocks = 2 * MOBA_GROUP
    last = blk // pair_blocks
    n_far = jnp.maximum(jnp.maximum(blk - (NEAR_BLOCKS - 1), 0) // pair_blocks - 1, 0)

    @pl.when(blk == 0)
    def _():
        score_group(0, buf0, True)

    state = lax.fori_loop(0, n_far, lambda i, st: group_pair(i, st, False), state)
    state = lax.fori_loop(n_far, last, lambda i, st: group_pair(i, st, True), state)

    second = (blk % pair_blocks) >= MOBA_GROUP
    score_group(2 * last + 1, buf1, True)
    state = group(2 * last, buf0, state)
    score_group(0, buf0, True, qnext_ref, blk + 1)
    state = lax.cond(second, lambda st: tuple(group(2 * last + 1, buf1, list(st))),
                     lambda st: tuple(st), tuple(state))

    accs = state[1::2]
    out_t = jnp.concatenate([a[:MOBA_HEAD_DIM] / a[MOBA_HEAD_DIM:MOBA_HEAD_DIM + 1] for a in accs], axis=0)
    out_ref[0] = out_t.T.astype(out_ref.dtype)


def _moba(qaug, k, vt, bias_tiles):
    b, _, s, _ = k.shape
    nb = s // MOBA_BLOCK
    assert nb % (2 * MOBA_GROUP) == 0
    hs = MOBA_HEADS_PER_STEP
    width = hs * MOBA_HEAD_DIM
    return pl.pallas_call(
        _moba_kernel,
        grid=(b, MOBA_HEADS // hs, nb),
        in_specs=[
            pl.BlockSpec((1, hs, 1, LANES, MOBA_BLOCK), lambda bi, pi, ti: (bi, pi, ti, 0, 0)),
            pl.BlockSpec((1, hs, 1, LANES, MOBA_BLOCK),
                         lambda bi, pi, ti: (bi, pi, jnp.minimum(ti + 1, nb - 1), 0, 0)),
            pl.BlockSpec((1, hs, s, LANES), lambda bi, pi, ti: (bi, pi, 0, 0)),
            pl.BlockSpec((1, hs, VT_ROWS, s), lambda bi, pi, ti: (bi, pi, 0, 0)),
            pl.BlockSpec((hs, NEAR_BLOCKS + 1, MOBA_BLOCK, MOBA_BLOCK),
                         lambda bi, pi, ti: (pi, 0, 0, 0), pipeline_mode=pl.Buffered(1)),
        ],
        out_specs=pl.BlockSpec((1, MOBA_BLOCK, width), lambda bi, pi, ti: (bi, ti, pi)),
        out_shape=jax.ShapeDtypeStruct((b, s, MOBA_WIDTH), BF16),
        scratch_shapes=[pltpu.VMEM((hs, MOBA_GROUP * MOBA_BLOCK, MOBA_BLOCK), F32),
                        pltpu.VMEM((hs, MOBA_GROUP * MOBA_BLOCK, MOBA_BLOCK), F32),
                        pltpu.VMEM((hs, SUBLANES, MOBA_BLOCK), F32),
                        pltpu.VMEM((hs, SUBLANES, MOBA_BLOCK), F32)],
        compiler_params=pltpu.CompilerParams(
            dimension_semantics=("arbitrary", "arbitrary", "arbitrary"), vmem_limit_bytes=VMEM_LIMIT_BYTES),
        name="moba_attention",
    )(qaug, qaug, k, vt, bias_tiles)


def _chunk_cumsum(g):
    groups = HGRN_CHUNK // SUBLANES
    g3 = g.reshape(groups, SUBLANES, g.shape[-1])
    sub = lax.broadcasted_iota(jnp.int32, g3.shape, 1)
    shift = 1
    while shift < SUBLANES:
        g3 = g3 + jnp.where(sub >= shift, pltpu.roll(g3, shift, axis=1), 0.0)
        shift *= 2
    rows = []
    run = None
    for i in range(groups):
        cur = g3[i] if run is None else g3[i] + run
        rows.append(cur)
        run = cur[SUBLANES - 1:SUBLANES, :]
    return jnp.concatenate(rows, axis=0)


def _hgrn_masks():
    c = HGRN_CHUNK
    row = lax.broadcasted_iota(jnp.int32, (c, 1), 0)
    rt = lax.broadcasted_iota(jnp.int32, (c, c), 0)
    cs = lax.broadcasted_iota(jnp.int32, (c, c), 1)
    levels = []
    for half in HGRN_LEVELS:
        shift = int(math.log2(2 * half))
        valid = ((rt >> shift) == (cs >> shift)) & ((rt & half) != 0) & ((cs & half) == 0)
        levels.append(((row & half) != 0, valid))
    return levels, rt == cs


def _hgrn_chunk_matmuls(q, k, v, g2, state_t, level_masks):
    c = HGRN_CHUNK
    cum = _chunk_cumsum(g2)
    last = cum[c - 1:c, :]
    v16 = v.astype(BF16)

    o = _dot_nt((q * jnp.exp2(cum)).astype(BF16), state_t.astype(BF16))
    k_end = (k * jnp.exp2(last - cum)).astype(BF16)
    new_state = state_t * jnp.exp2(last) + _dot_tn(v16, k_end)

    scores = jnp.zeros((c, c), F32)
    groups = c // SUBLANES
    width = cum.shape[-1]
    cum3 = cum.reshape(groups, SUBLANES, width)
    sub = lax.broadcasted_iota(jnp.int32, (groups, SUBLANES, width), 1)

    def row_of_group(r):
        return jnp.broadcast_to(cum3[:, r:r + 1, :], cum3.shape)

    for half, (is_q, valid) in zip(HGRN_LEVELS, level_masks):
        if half == 1:
            exponent = jnp.where(is_q, g2, 0.0)
        else:
            if half >= SUBLANES:
                ref = jnp.concatenate(
                    [jnp.broadcast_to(cum[lo + half - 1:lo + half, :], (2 * half, width))
                     for lo in range(0, c, 2 * half)], axis=0)
            elif half == 4:
                ref = row_of_group(3).reshape(c, width)
            else:
                ref = jnp.where(sub < 4, row_of_group(1), row_of_group(5)).reshape(c, width)
            exponent = jnp.where(is_q, cum - ref, ref - cum)
        x = (jnp.where(is_q, q, k) * jnp.exp2(exponent)).astype(BF16)
        scores = scores + jnp.where(valid, _dot_nt(x, x), 0.0)
    return o, new_state, scores, v16


def _hgrn_kernel(q_ref, k_ref, v_ref, g_ref, out_ref, state_ref):
    @pl.when(pl.program_id(1) == 0)
    def _():
        state_ref[...] = jnp.zeros_like(state_ref)

    n_chunks = q_ref.shape[1] // HGRN_CHUNK
    masks = _hgrn_masks()

    level_masks, on_diagonal = masks

    def chunk(ci, carry):
        rows = pl.ds(pl.multiple_of(ci * HGRN_CHUNK, HGRN_CHUNK), HGRN_CHUNK)
        partial = []
        for h in range(HGRN_HEADS):
            cols = slice(h * HGRN_DIM, (h + 1) * HGRN_DIM)
            q, k = q_ref[0, rows, cols], k_ref[0, rows, cols]
            o, new_state, scores, v16 = _hgrn_chunk_matmuls(
                q, k, v_ref[0, rows, cols], g_ref[0, rows, cols], state_ref[h], level_masks)
            state_ref[h] = new_state
            partial.append((jnp.sum(q * k, axis=-1, keepdims=True), o, scores, v16))
        for h, (self_score, o, scores, v16) in enumerate(partial):
            scores = jnp.where(on_diagonal, self_score, scores)
            out = o + _dot(scores.astype(BF16), v16)
            out_ref[0, rows, h * HGRN_DIM:(h + 1) * HGRN_DIM] = out.astype(out_ref.dtype)
        return carry

    lax.fori_loop(0, n_chunks, chunk, 0, unroll=8)


def _hgrn(hq, hk, hv, hg):
    b, s, w = hq.shape
    ts = min(s, 512)
    spec = pl.BlockSpec((1, ts, w), lambda bi, ti: (bi, ti, 0))
    return pl.pallas_call(
        _hgrn_kernel,
        grid=(b, s // ts),
        in_specs=[spec, spec, spec, spec],
        out_specs=spec,
        out_shape=jax.ShapeDtypeStruct((b, s, w), BF16),
        scratch_shapes=[pltpu.VMEM((HGRN_HEADS, HGRN_DIM, HGRN_DIM), F32)],
        compiler_params=pltpu.CompilerParams(
            dimension_semantics=("arbitrary", "arbitrary"), vmem_limit_bytes=VMEM_LIMIT_BYTES),
        name="hgrn2_recurrence",
    )(hq, hk, hv, hg)


def _merge_kernel(h_ref, ya_ref, ag_ref, yb_ref, bg_ref, ga_ref, gb_ref, p_ref, og_ref,
                  wa_ref, wb_ref, wo_ref, wp_ref, wg_ref, out_ref):
    ya = _dot(ya_ref[0] * ag_ref[0], wa_ref[0])

    yb = yb_ref[0].astype(F32)
    normed = []
    for h in range(HGRN_HEADS):
        t = yb[:, h * HGRN_DIM:(h + 1) * HGRN_DIM]
        normed.append(t * lax.rsqrt(jnp.mean(t * t, axis=-1, keepdims=True) + RMS_EPS))
    yb = jnp.concatenate(normed, axis=-1) * og_ref[...]
    yb = _dot((yb * bg_ref[0]).astype(BF16), wb_ref[0])

    merged = ga_ref[0] * ya + gb_ref[0] * yb
    h = h_ref[0] + _dot(merged.astype(BF16), wo_ref[0])
    ple = _dot(p_ref[0, 0].astype(BF16), wp_ref[0]) * _sigmoid(_dot(h.astype(BF16), wg_ref[0]))
    out_ref[0] = h + ple


def _merge(layer, h, ya, ag, yb, bg, ga, gb, p, out_gain, w_up_a, w_up_b, w_out, w_ple, w_ple_gate):
    b, s, d = h.shape
    tm = min(s, 512)
    tok = lambda width: pl.BlockSpec((1, tm, width), lambda bi, ti: (bi, ti, 0))
    wspec = lambda w: pl.BlockSpec((1,) + w.shape[1:], lambda bi, ti: (layer, 0, 0), pipeline_mode=pl.Buffered(1))
    return pl.pallas_call(
        _merge_kernel,
        grid=(b, s // tm),
        in_specs=[
            tok(d), tok(MOBA_WIDTH), tok(MOBA_WIDTH), tok(HGRN_WIDTH), tok(HGRN_WIDTH), tok(d), tok(d),
            pl.BlockSpec((1, 1, tm, p.shape[-1]), lambda bi, ti: (layer, bi, ti, 0)),
            pl.BlockSpec((1, HGRN_WIDTH), lambda bi, ti: (0, 0)),
            wspec(w_up_a), wspec(w_up_b), wspec(w_out), wspec(w_ple), wspec(w_ple_gate),
        ],
        out_specs=tok(d),
        out_shape=jax.ShapeDtypeStruct((b, s, d), F32),
        compiler_params=pltpu.CompilerParams(
            dimension_semantics=("arbitrary", "arbitrary"), vmem_limit_bytes=VMEM_LIMIT_BYTES),
        name=f"merge_l{layer}",
    )(h, ya, ag, yb, bg, ga, gb, p, out_gain, w_up_a, w_up_b, w_out, w_ple, w_ple_gate)


def kernel(x, p, norm_gain, w_in, q_norm_gain, k_norm_gain, rel_bias, hgrn_lb_logits, hgrn_out_gain,
           w_up_a, w_up_b, w_out, w_ple, w_ple_gate):
    b, s, d = x.shape
    depth = w_in.shape[0]
    assert s % MOBA_BLOCK == 0 and s % HGRN_CHUNK == 0

    w_in, w_up_a, w_up_b, w_out, w_ple, w_ple_gate = (
        w.astype(BF16) for w in (w_in, w_up_a, w_up_b, w_out, w_ple, w_ple_gate))
    head_id = jnp.arange(MOBA_WIDTH) // MOBA_HEAD_DIM
    hmean = ((head_id[:, None] == head_id[None, :]).astype(F32) / MOBA_HEAD_DIM).astype(BF16)
    tile = lambda g: jnp.tile(g.astype(F32), MOBA_HEADS)[None, :]

    bias_tiles = _bias_tiles(rel_bias)
    h = x
    for i in range(depth):
        qaug, k, vt, ag, hq, hk, hv, hg, bg, ga, gb = _in_proj(
            i, rel_bias, h, norm_gain[i][None, :].astype(F32), w_in, tile(q_norm_gain[i]), tile(k_norm_gain[i]),
            hgrn_lb_logits.astype(F32), hmean)
        ya = _moba(qaug, k, vt, bias_tiles)
        yb = _hgrn(hq, hk, hv, hg)
        h = _merge(i, h, ya, ag, yb, bg, ga, gb, p, hgrn_out_gain[i][None, :].astype(F32),
                   w_up_a, w_up_b, w_out, w_ple, w_ple_gate)
    return h
```

```python
import functools
import math

import jax
import jax.numpy as jnp
from jax import lax
from jax.experimental import pallas as pl
from jax.experimental.pallas import tpu as pltpu

F32 = jnp.float32
BF16 = jnp.bfloat16

MOBA_HEADS = 8
MOBA_HEAD_DIM = 64
MOBA_WIDTH = MOBA_HEADS * MOBA_HEAD_DIM
MOBA_BLOCK = 256
MOBA_TOPK = 3
HGRN_HEADS = 4
HGRN_DIM = 128
HGRN_WIDTH = HGRN_HEADS * HGRN_DIM
EXP_CLIP = 30.0
REL_BUCKETS = 32
REL_MAX_DIST = 2048
RMS_EPS = 1e-6
MASK_VALUE = -1e30

LANES = 128
SUBLANES = 8
VMEM_LIMIT_BYTES = 56 * 1024 * 1024

LOG2E = math.log2(math.e)
HEADS_PER_PAIR = LANES // MOBA_HEAD_DIM
MOBA_PAIRS_PER_STEP = 2
MOBA_HEADS_PER_STEP = HEADS_PER_PAIR * MOBA_PAIRS_PER_STEP
MOBA_GROUP = 2
MASK_LANES = 32
IN_PROJ_TILE = 512
VT_ROWS = MOBA_HEAD_DIM + 16

_MAX_EXACT = REL_BUCKETS // 2
_LAST_BUCKET_DIST = math.ceil(_MAX_EXACT * (REL_MAX_DIST / _MAX_EXACT) ** ((REL_BUCKETS - 1 - _MAX_EXACT) / (REL_BUCKETS - _MAX_EXACT)))
NEAR_BLOCKS = -(-(_LAST_BUCKET_DIST + MOBA_BLOCK - 1) // MOBA_BLOCK)

HGRN_CHUNK = 64
HGRN_LEVELS = (32, 16, 8, 4, 2, 1)


def _sigmoid(x):
    return 0.5 * jnp.tanh(0.5 * x) + 0.5


def _silu(x):
    return x * _sigmoid(x)


def _dot(a, b):
    return jnp.dot(a, b, preferred_element_type=F32)


def _dot_nt(a, b):
    return lax.dot_general(a, b, (((1,), (1,)), ((), ())), preferred_element_type=F32)


def _dot_tn(a, b):
    return lax.dot_general(a, b, (((0,), (0,)), ((), ())), preferred_element_type=F32)


def _bias_tiles_kernel(tab_ref, out_ref):
    h = pl.program_id(0)
    d = pl.program_id(1)
    span = 2 * MOBA_BLOCK
    lane = lax.broadcasted_iota(jnp.int32, (SUBLANES, span), 1)
    rel = d * MOBA_BLOCK + jnp.where(lane < MOBA_BLOCK, lane, lane - span)
    n = jnp.maximum(rel, 0)
    nf = jnp.maximum(n, _MAX_EXACT).astype(F32)
    large = _MAX_EXACT + (jnp.log(nf / _MAX_EXACT) / math.log(REL_MAX_DIST / _MAX_EXACT)
                          * (REL_BUCKETS - _MAX_EXACT)).astype(jnp.int32)
    large = jnp.minimum(large, REL_BUCKETS - 1)
    bucket = jnp.where(n < _MAX_EXACT, n, large)
    val = jnp.zeros((SUBLANES, span), F32)
    for b in range(REL_BUCKETS):
        val = jnp.where(bucket == b, tab_ref[b, h], val)
    val = jnp.where(rel >= 0, val * LOG2E, MASK_VALUE)
    val = jnp.where(d < NEAR_BLOCKS, val, 0.0)
    rows = jnp.broadcast_to(val[0:1], (MOBA_BLOCK, span))
    out_ref[0, 0] = pltpu.roll(rows, 0, 1, stride=1, stride_axis=0)[:, :MOBA_BLOCK]


def _bias_tiles(rel_bias):
    return pl.pallas_call(
        _bias_tiles_kernel,
        grid=(MOBA_HEADS, NEAR_BLOCKS + 1),
        in_specs=[pl.BlockSpec(memory_space=pltpu.SMEM)],
        out_specs=pl.BlockSpec((1, 1, MOBA_BLOCK, MOBA_BLOCK), lambda h, d: (h, d, 0, 0)),
        out_shape=jax.ShapeDtypeStruct((MOBA_HEADS, NEAR_BLOCKS + 1, MOBA_BLOCK, MOBA_BLOCK), F32),
        name="t5_bias_tiles",
    )(rel_bias.astype(F32))


def _split_bf16(x):
    hi = x.astype(BF16)
    lo = (x - hi.astype(F32)).astype(BF16)
    return hi, lo


def _moba_queries(blk, slot, qn, kmean, tab_ref, qaug_ref):
    tq = MOBA_BLOCK
    scale = MOBA_HEAD_DIM ** -0.5 * LOG2E
    qt = qn.T
    lane = lax.broadcasted_iota(jnp.int32, (MASK_LANES, LANES), 1)
    kblk = lax.broadcasted_iota(jnp.int32, (MASK_LANES, tq), 0)
    kblk_f = kblk.astype(F32)
    past = kblk < blk
    for pp in range(MOBA_HEADS // HEADS_PER_PAIR):
        qt_pair = qt[pp * LANES:(pp + 1) * LANES]
        q_hi, q_lo = _split_bf16(qt_pair)
        km_pair = kmean[:, pp * LANES:(pp + 1) * LANES]
        for hh in range(HEADS_PER_PAIR):
            h = pp * HEADS_PER_PAIR + hh
            in_head = (lane >= hh * MOBA_HEAD_DIM) & (lane < (hh + 1) * MOBA_HEAD_DIM)
            k_hi, k_lo = _split_bf16(jnp.where(in_head, km_pair, 0.0))
            gate = _dot(k_hi, q_hi) + _dot(k_hi, q_lo) + _dot(k_lo, q_hi)
            gate = jnp.where(past, gate, -jnp.inf)
            sel = jnp.zeros((MASK_LANES, tq), jnp.bool_)
            for _ in range(MOBA_TOPK):
                best = jnp.max(gate, axis=0, keepdims=True)
                first = jnp.min(jnp.where(gate == best, kblk_f, float(MASK_LANES)), axis=0, keepdims=True)
                hit = kblk_f == first
                sel = sel | hit
                gate = jnp.where(hit, -jnp.inf, gate)
            sel = (sel & past) | (kblk == blk)
            far_bias = jnp.where(blk - kblk >= NEAR_BLOCKS, tab_ref[REL_BUCKETS - 1, h] * LOG2E, 0.0)
            mask_hi, mask_lo = _split_bf16(jnp.where(sel, far_bias, MASK_VALUE))
            q_rows = (qt_pair[hh * MOBA_HEAD_DIM:(hh + 1) * MOBA_HEAD_DIM] * scale).astype(BF16)
            parts = [q_rows, mask_hi, mask_lo] if hh == 0 else [mask_hi, mask_lo, q_rows]
            qaug_ref[0, h, slot] = jnp.concatenate(parts, axis=0)


def _in_proj_kernel(layer, tab_ref, x_ref, gain_ref, w_ref, qg_ref, kg_ref, lbl_ref, hmean_ref,
                    qaug_ref, k_ref, vt_ref, ag_ref, hq_ref, hk_ref, hv_ref, hg_ref,
                    bg_ref, ga_ref, gb_ref, kmean_ref):
    tile_blocks = x_ref.shape[1] // MOBA_BLOCK
    first_blk = pl.program_id(1) * tile_blocks
    x = x_ref[0]
    xn = x * lax.rsqrt(jnp.mean(x * x, axis=-1, keepdims=True) + RMS_EPS) * gain_ref[...]
    xn = xn.astype(BF16)

    def seg(start, width):
        return _dot(xn, w_ref[0, :, start:start + width])

    def head_rms(t, gain):
        ms = _dot((t * t).astype(BF16), hmean_ref[...])
        return t * lax.rsqrt(ms + RMS_EPS) * gain

    w = MOBA_WIDTH
    qn = head_rms(seg(0, w), qg_ref[...])
    kn = head_rms(seg(w, w), kg_ref[...])

    @pl.when(first_blk == 0)
    def _():
        kmean_ref[...] = jnp.zeros_like(kmean_ref)

    for j in range(tile_blocks):
        rows = slice(j * MOBA_BLOCK, (j + 1) * MOBA_BLOCK)
        _moba_queries(first_blk + j, j, qn[rows], kmean_ref[...], tab_ref, qaug_ref)
        kmean_ref[pl.ds(first_blk + j, 1), :] = jnp.mean(kn[rows], axis=0, keepdims=True)
    lane = lax.broadcasted_iota(jnp.int32, (x.shape[0], LANES), 1)
    row = lax.broadcasted_iota(jnp.int32, (x.shape[0], LANES), 0)
    row_blk = first_blk + jnp.right_shift(row, MOBA_BLOCK.bit_length() - 1)
    block_hot = jnp.where((lane & (MASK_LANES - 1)) == row_blk, 1.0, 0.0)
    for pp in range(MOBA_HEADS // HEADS_PER_PAIR):
        kp = kn[:, pp * LANES:(pp + 1) * LANES]
        for hh, own_lanes in enumerate((lane < MOBA_HEAD_DIM, lane >= MOBA_HEAD_DIM)):
            k_head = jnp.where(own_lanes, kp, block_hot).astype(BF16)
            k_ref[0, HEADS_PER_PAIR * pp + hh] = pltpu.bitcast(k_head, jnp.uint32)
    vt = seg(2 * w, w).T
    extra = lax.broadcasted_iota(jnp.int32, (VT_ROWS - MOBA_HEAD_DIM, x.shape[0]), 0)
    ones_row = jnp.where(extra == 0, 1.0, 0.0).astype(BF16)
    for h in range(MOBA_HEADS):
        v_head = vt[h * MOBA_HEAD_DIM:(h + 1) * MOBA_HEAD_DIM].astype(BF16)
        vt_ref[0, h, :MOBA_HEAD_DIM // 2] = pltpu.bitcast(v_head, jnp.uint32)
        vt_ref[0, h, MOBA_HEAD_DIM // 2:] = pltpu.bitcast(ones_row, jnp.uint32)
    ag_ref[0] = _silu(seg(3 * w, w)).astype(BF16)

    base = 4 * w
    hw = HGRN_WIDTH
    hq_ref[0] = _silu(seg(base, hw))
    lbl = lbl_ref[...]
    e = jnp.exp(lbl - jnp.max(lbl, axis=0, keepdims=True))
    sm = e / jnp.sum(e, axis=0, keepdims=True)
    lb = jnp.zeros((1, hw), F32)
    for j in range(1, layer + 1):
        lb = lb + sm[j:j + 1, :]
    z = seg(base + hw, hw)
    e = jnp.exp(-jnp.abs(z))
    pos = z >= 0.0
    inv = 1.0 / (1.0 + e)
    clipped = jnp.where(pos, e, jnp.minimum(1.0 / e, math.exp(EXP_CLIP)))
    hg_ref[0] = jnp.minimum(z, 0.0) * LOG2E + jnp.log2((1.0 + lb * clipped) * inv)
    hk_ref[0] = (1.0 - lb) * jnp.where(pos, e, 1.0) * inv
    hv_ref[0] = seg(base + 2 * hw, hw)
    bg_ref[0] = _silu(seg(base + 3 * hw, hw)).astype(BF16)

    base = base + 4 * hw
    d = x.shape[-1]
    ga_ref[0] = _sigmoid(seg(base, d)).astype(BF16)
    gb_ref[0] = _sigmoid(seg(base + d, d)).astype(BF16)


def _in_proj(layer, rel_bias, h, norm_gain, w_in, q_gain, k_gain, lb_logits, hmean):
    b, s, d = h.shape
    depth = w_in.shape[0]
    tm = IN_PROJ_TILE
    nt = s // tm
    tile_blocks = tm // MOBA_BLOCK
    nb = s // MOBA_BLOCK
    assert s % tm == 0 and nb <= MASK_LANES and HEADS_PER_PAIR == 2
    in_width = w_in.shape[-1]
    tok = lambda width: pl.BlockSpec((1, tm, width), lambda bi, ti: (bi, ti, 0))
    const2 = lambda shape: pl.BlockSpec(shape, lambda bi, ti: (0, 0))
    tok_shape = lambda width, dtype: jax.ShapeDtypeStruct((b, s, width), dtype)
    outs = (
        (pl.BlockSpec((1, MOBA_HEADS, tile_blocks, LANES, MOBA_BLOCK), lambda bi, ti: (bi, 0, ti, 0, 0)),
         jax.ShapeDtypeStruct((b, MOBA_HEADS, nb, LANES, MOBA_BLOCK), BF16)),
        (pl.BlockSpec((1, MOBA_HEADS, tm // 2, LANES), lambda bi, ti: (bi, 0, ti, 0)),
         jax.ShapeDtypeStruct((b, MOBA_HEADS, s // 2, LANES), jnp.uint32)),
        (pl.BlockSpec((1, MOBA_HEADS, VT_ROWS // 2, tm), lambda bi, ti: (bi, 0, 0, ti)),
         jax.ShapeDtypeStruct((b, MOBA_HEADS, VT_ROWS // 2, s), jnp.uint32)),
        (tok(MOBA_WIDTH), tok_shape(MOBA_WIDTH, BF16)),
        (tok(HGRN_WIDTH), tok_shape(HGRN_WIDTH, F32)),
        (tok(HGRN_WIDTH), tok_shape(HGRN_WIDTH, F32)),
        (tok(HGRN_WIDTH), tok_shape(HGRN_WIDTH, F32)),
        (tok(HGRN_WIDTH), tok_shape(HGRN_WIDTH, F32)),
        (tok(HGRN_WIDTH), tok_shape(HGRN_WIDTH, BF16)),
        (tok(d), tok_shape(d, BF16)),
        (tok(d), tok_shape(d, BF16)),
    )
    return pl.pallas_call(
        functools.partial(_in_proj_kernel, layer),
        grid=(b, nt),
        in_specs=[
            pl.BlockSpec(memory_space=pltpu.SMEM),
            tok(d),
            const2((1, d)),
            pl.BlockSpec((1, d, in_width), lambda bi, ti: (layer, 0, 0), pipeline_mode=pl.Buffered(1)),
            const2((1, MOBA_WIDTH)),
            const2((1, MOBA_WIDTH)),
            const2((depth, HGRN_WIDTH)),
            const2((MOBA_WIDTH, MOBA_WIDTH)),
        ],
        out_specs=[o[0] for o in outs],
        out_shape=[o[1] for o in outs],
        scratch_shapes=[pltpu.VMEM((MASK_LANES, MOBA_WIDTH), F32)],
        compiler_params=pltpu.CompilerParams(
            dimension_semantics=("arbitrary", "arbitrary"), vmem_limit_bytes=VMEM_LIMIT_BYTES),
        name=f"in_proj_l{layer}",
    )(rel_bias.astype(F32), h, norm_gain, w_in, q_gain, k_gain, lb_logits, hmean)


def _moba_kernel(qaug_ref, qnext_ref, k_ref, vt_ref, bias_ref, out_ref, s0_ref, s1_ref, max0_ref, max1_ref):
    blk = pl.program_id(2)
    tq = MOBA_BLOCK
    group_rows = MOBA_GROUP * MOBA_BLOCK

    def score_head(g, h, bufs, near, q_ref=qaug_ref, q_blk=blk):
        s_ref, max_ref = bufs
        words = pl.ds(pl.multiple_of(g * (group_rows // 2), group_rows // 2), group_rows // 2)
        k_rows = pltpu.bitcast(k_ref[0, h, words, :], BF16)
        s = _dot(k_rows, q_ref[0, h, 0])
        if near:
            parts = []
            for u in range(MOBA_GROUP):
                dist = q_blk - (g * MOBA_GROUP + u)
                tile = bias_ref[h, jnp.where((dist >= 0) & (dist < NEAR_BLOCKS), dist, NEAR_BLOCKS)]
                parts.append(s[u * MOBA_BLOCK:(u + 1) * MOBA_BLOCK] + tile)
            s = jnp.concatenate(parts, axis=0)
        s_ref[h] = s
        max_ref[h] = jnp.max(s.reshape(group_rows // SUBLANES, SUBLANES, tq), axis=0)

    def softmax_head(g, h, bufs, m, acc):
        s_ref, max_ref = bufs
        rows = pl.ds(pl.multiple_of(g * group_rows, group_rows), group_rows)
        m_new = jnp.maximum(m, jnp.max(max_ref[h], axis=0, keepdims=True))
        p = jnp.exp2(s_ref[h] - m_new).astype(BF16)
        acc = jnp.exp2(m - m_new) * acc + _dot(pltpu.bitcast(vt_ref[0, h, :, rows], BF16), p)
        return m_new, acc

    heads = range(MOBA_HEADS_PER_STEP)
    state = []
    for _ in heads:
        state += [jnp.full((1, tq), MASK_VALUE, F32), jnp.zeros((VT_ROWS, tq), F32)]
    bufs = ((s0_ref, max0_ref), (s1_ref, max1_ref))

    def group_pair(i, state, near_next):
        state = list(state)
        for parity in range(2):
            for h in heads:
                state[2 * h], state[2 * h + 1] = softmax_head(
                    2 * i + parity, h, bufs[parity], state[2 * h], state[2 * h + 1])
                score_head(2 * i + 2 + parity, h, bufs[parity], near_next)
        return state

    pair_blocks = 2 * MOBA_GROUP
    last = blk // pair_blocks
    n_far = jnp.maximum(jnp.maximum(blk - (NEAR_BLOCKS - 1), 0) // pair_blocks - 1, 0)

    @pl.when(blk == 0)
    def _():
        for parity in range(2):
            for h in heads:
                score_head(parity, h, bufs[parity], True)

    state = lax.fori_loop(0, n_far, lambda i, st: group_pair(i, st, False), state)
    state = lax.fori_loop(n_far, last, lambda i, st: group_pair(i, st, True), state)

    state = list(state)
    for parity in range(2):
        for h in heads:
            state[2 * h], state[2 * h + 1] = softmax_head(
                2 * last + parity, h, bufs[parity], state[2 * h], state[2 * h + 1])
            score_head(parity, h, bufs[parity], True, qnext_ref, blk + 1)

    accs = state[1::2]
    out_t = jnp.concatenate([a[:MOBA_HEAD_DIM] / a[MOBA_HEAD_DIM:MOBA_HEAD_DIM + 1] for a in accs], axis=0)
    out_ref[0] = out_t.T.astype(out_ref.dtype)


def _moba(qaug, k, vt, bias_tiles):
    b, _, _, s = vt.shape
    nb = s // MOBA_BLOCK
    assert nb % (2 * MOBA_GROUP) == 0
    hs = MOBA_HEADS_PER_STEP
    width = hs * MOBA_HEAD_DIM
    return pl.pallas_call(
        _moba_kernel,
        grid=(b, MOBA_HEADS // hs, nb),
        in_specs=[
            pl.BlockSpec((1, hs, 1, LANES, MOBA_BLOCK), lambda bi, pi, ti: (bi, pi, ti, 0, 0)),
            pl.BlockSpec((1, hs, 1, LANES, MOBA_BLOCK),
                         lambda bi, pi, ti: (bi, pi, jnp.minimum(ti + 1, nb - 1), 0, 0)),
            pl.BlockSpec((1, hs, s // 2, LANES), lambda bi, pi, ti: (bi, pi, 0, 0)),
            pl.BlockSpec((1, hs, VT_ROWS // 2, s), lambda bi, pi, ti: (bi, pi, 0, 0)),
            pl.BlockSpec((hs, NEAR_BLOCKS + 1, MOBA_BLOCK, MOBA_BLOCK),
                         lambda bi, pi, ti: (pi, 0, 0, 0), pipeline_mode=pl.Buffered(1)),
        ],
        out_specs=pl.BlockSpec((1, MOBA_BLOCK, width), lambda bi, pi, ti: (bi, ti, pi)),
        out_shape=jax.ShapeDtypeStruct((b, s, MOBA_WIDTH), BF16),
        scratch_shapes=[pltpu.VMEM((hs, MOBA_GROUP * MOBA_BLOCK, MOBA_BLOCK), F32),
                        pltpu.VMEM((hs, MOBA_GROUP * MOBA_BLOCK, MOBA_BLOCK), F32),
                        pltpu.VMEM((hs, SUBLANES, MOBA_BLOCK), F32),
                        pltpu.VMEM((hs, SUBLANES, MOBA_BLOCK), F32)],
        compiler_params=pltpu.CompilerParams(
            dimension_semantics=("arbitrary", "arbitrary", "arbitrary"), vmem_limit_bytes=VMEM_LIMIT_BYTES),
        name="moba_attention",
    )(qaug, qaug, k, vt, bias_tiles)


def _chunk_cumsum(g):
    groups = HGRN_CHUNK // SUBLANES
    g3 = g.reshape(groups, SUBLANES, g.shape[-1])
    sub = lax.broadcasted_iota(jnp.int32, g3.shape, 1)
    shift = 1
    while shift < SUBLANES:
        g3 = g3 + jnp.where(sub >= shift, pltpu.roll(g3, shift, axis=1), 0.0)
        shift *= 2
    rows = []
    run = None
    for i in range(groups):
        cur = g3[i] if run is None else g3[i] + run
        rows.append(cur)
        run = cur[SUBLANES - 1:SUBLANES, :]
    return jnp.concatenate(rows, axis=0)


def _hgrn_masks():
    c = HGRN_CHUNK
    row = lax.broadcasted_iota(jnp.int32, (c, 1), 0)
    rt = lax.broadcasted_iota(jnp.int32, (c, c), 0)
    cs = lax.broadcasted_iota(jnp.int32, (c, c), 1)
    levels = []
    for half in HGRN_LEVELS:
        shift = int(math.log2(2 * half))
        valid = ((rt >> shift) == (cs >> shift)) & ((rt & half) != 0) & ((cs & half) == 0)
        levels.append(((row & half) != 0, valid))
    return levels, rt == cs


def _hgrn_chunk_matmuls(q, k, v, g2, state_t, level_masks):
    c = HGRN_CHUNK
    cum = _chunk_cumsum(g2)
    last = cum[c - 1:c, :]
    v16 = v.astype(BF16)

    o = _dot_nt((q * jnp.exp2(cum)).astype(BF16), state_t.astype(BF16))
    k_end = (k * jnp.exp2(last - cum)).astype(BF16)
    new_state = state_t * jnp.exp2(last) + _dot_tn(v16, k_end)

    scores = jnp.zeros((c, c), F32)
    groups = c // SUBLANES
    width = cum.shape[-1]
    cum3 = cum.reshape(groups, SUBLANES, width)
    sub = lax.broadcasted_iota(jnp.int32, (groups, SUBLANES, width), 1)

    def row_of_group(r):
        return jnp.broadcast_to(cum3[:, r:r + 1, :], cum3.shape)

    for half, (is_q, valid) in zip(HGRN_LEVELS, level_masks):
        if half == 1:
            exponent = jnp.where(is_q, g2, 0.0)
        else:
            if half >= SUBLANES:
                ref = jnp.concatenate(
                    [jnp.broadcast_to(cum[lo + half - 1:lo + half, :], (2 * half, width))
                     for lo in range(0, c, 2 * half)], axis=0)
            elif half == 4:
                ref = row_of_group(3).reshape(c, width)
            else:
                ref = jnp.where(sub < 4, row_of_group(1), row_of_group(5)).reshape(c, width)
            exponent = jnp.where(is_q, cum - ref, ref - cum)
        x = (jnp.where(is_q, q, k) * jnp.exp2(exponent)).astype(BF16)
        scores = scores + jnp.where(valid, _dot_nt(x, x), 0.0)
    return o, new_state, scores, v16


def _hgrn_kernel(q_ref, k_ref, v_ref, g_ref, out_ref, state_ref):
    @pl.when(pl.program_id(1) == 0)
    def _():
        state_ref[...] = jnp.zeros_like(state_ref)

    n_chunks = q_ref.shape[1] // HGRN_CHUNK
    masks = _hgrn_masks()

    level_masks, on_diagonal = masks

    def chunk(ci, carry):
        rows = pl.ds(pl.multiple_of(ci * HGRN_CHUNK, HGRN_CHUNK), HGRN_CHUNK)
        partial = []
        for h in range(HGRN_HEADS):
            cols = slice(h * HGRN_DIM, (h + 1) * HGRN_DIM)
            q, k = q_ref[0, rows, cols], k_ref[0, rows, cols]
            o, new_state, scores, v16 = _hgrn_chunk_matmuls(
                q, k, v_ref[0, rows, cols], g_ref[0, rows, cols], state_ref[h], level_masks)
            state_ref[h] = new_state
            partial.append((jnp.sum(q * k, axis=-1, keepdims=True), o, scores, v16))
        for h, (self_score, o, scores, v16) in enumerate(partial):
            scores = jnp.where(on_diagonal, self_score, scores)
            out = o + _dot(scores.astype(BF16), v16)
            out_ref[0, rows, h * HGRN_DIM:(h + 1) * HGRN_DIM] = out.astype(out_ref.dtype)
        return carry

    lax.fori_loop(0, n_chunks, chunk, 0, unroll=8)


def _hgrn(hq, hk, hv, hg):
    b, s, w = hq.shape
    ts = min(s, 512)
    spec = pl.BlockSpec((1, ts, w), lambda bi, ti: (bi, ti, 0))
    return pl.pallas_call(
        _hgrn_kernel,
        grid=(b, s // ts),
        in_specs=[spec, spec, spec, spec],
        out_specs=spec,
        out_shape=jax.ShapeDtypeStruct((b, s, w), BF16),
        scratch_shapes=[pltpu.VMEM((HGRN_HEADS, HGRN_DIM, HGRN_DIM), F32)],
        compiler_params=pltpu.CompilerParams(
            dimension_semantics=("arbitrary", "arbitrary"), vmem_limit_bytes=VMEM_LIMIT_BYTES),
        name="hgrn2_recurrence",
    )(hq, hk, hv, hg)


def _merge_kernel(h_ref, ya_ref, ag_ref, yb_ref, bg_ref, ga_ref, gb_ref, p_ref, og_ref,
                  wa_ref, wb_ref, wo_ref, wp_ref, wg_ref, out_ref):
    ya = _dot(ya_ref[0] * ag_ref[0], wa_ref[0])

    yb = yb_ref[0].astype(F32)
    normed = []
    for h in range(HGRN_HEADS):
        t = yb[:, h * HGRN_DIM:(h + 1) * HGRN_DIM]
        normed.append(t * lax.rsqrt(jnp.mean(t * t, axis=-1, keepdims=True) + RMS_EPS))
    yb = jnp.concatenate(normed, axis=-1) * og_ref[...]
    yb = _dot((yb * bg_ref[0]).astype(BF16), wb_ref[0])

    merged = ga_ref[0] * ya + gb_ref[0] * yb
    h = h_ref[0] + _dot(merged.astype(BF16), wo_ref[0])
    ple = _dot(p_ref[0, 0].astype(BF16), wp_ref[0]) * _sigmoid(_dot(h.astype(BF16), wg_ref[0]))
    out_ref[0] = h + ple


def _merge(layer, h, ya, ag, yb, bg, ga, gb, p, out_gain, w_up_a, w_up_b, w_out, w_ple, w_ple_gate):
    b, s, d = h.shape
    tm = min(s, 512)
    tok = lambda width: pl.BlockSpec((1, tm, width), lambda bi, ti: (bi, ti, 0))
    wspec = lambda w: pl.BlockSpec((1,) + w.shape[1:], lambda bi, ti: (layer, 0, 0), pipeline_mode=pl.Buffered(1))
    return pl.pallas_call(
        _merge_kernel,
        grid=(b, s // tm),
        in_specs=[
            tok(d), tok(MOBA_WIDTH), tok(MOBA_WIDTH), tok(HGRN_WIDTH), tok(HGRN_WIDTH), tok(d), tok(d),
            pl.BlockSpec((1, 1, tm, p.shape[-1]), lambda bi, ti: (layer, bi, ti, 0)),
            pl.BlockSpec((1, HGRN_WIDTH), lambda bi, ti: (0, 0)),
            wspec(w_up_a), wspec(w_up_b), wspec(w_out), wspec(w_ple), wspec(w_ple_gate),
        ],
        out_specs=tok(d),
        out_shape=jax.ShapeDtypeStruct((b, s, d), F32),
        compiler_params=pltpu.CompilerParams(
            dimension_semantics=("arbitrary", "arbitrary"), vmem_limit_bytes=VMEM_LIMIT_BYTES),
        name=f"merge_l{layer}",
    )(h, ya, ag, yb, bg, ga, gb, p, out_gain, w_up_a, w_up_b, w_out, w_ple, w_ple_gate)


def kernel(x, p, norm_gain, w_in, q_norm_gain, k_norm_gain, rel_bias, hgrn_lb_logits, hgrn_out_gain,
           w_up_a, w_up_b, w_out, w_ple, w_ple_gate):
    b, s, d = x.shape
    depth = w_in.shape[0]
    assert s % MOBA_BLOCK == 0 and s % HGRN_CHUNK == 0

    w_in, w_up_a, w_up_b, w_out, w_ple, w_ple_gate = (
        w.astype(BF16) for w in (w_in, w_up_a, w_up_b, w_out, w_ple, w_ple_gate))
    head_id = jnp.arange(MOBA_WIDTH) // MOBA_HEAD_DIM
    hmean = ((head_id[:, None] == head_id[None, :]).astype(F32) / MOBA_HEAD_DIM).astype(BF16)
    tile = lambda g: jnp.tile(g.astype(F32), MOBA_HEADS)[None, :]

    bias_tiles = _bias_tiles(rel_bias)
    h = x
    for i in range(depth):
        qaug, k, vt, ag, hq, hk, hv, hg, bg, ga, gb = _in_proj(
            i, rel_bias, h, norm_gain[i][None, :].astype(F32), w_in, tile(q_norm_gain[i]), tile(k_norm_gain[i]),
            hgrn_lb_logits.astype(F32), hmean)
        ya = _moba(qaug, k, vt, bias_tiles)
        yb = _hgrn(hq, hk, hv, hg)
        h = _merge(i, h, ya, ag, yb, bg, ga, gb, p, hgrn_out_gain[i][None, :].astype(F32),
                   w_up_a, w_up_b, w_out, w_ple, w_ple_gate)
    return h
```

```python
import functools
import math

import jax
import jax.numpy as jnp
from jax import lax
from jax.experimental import pallas as pl
from jax.experimental.pallas import tpu as pltpu

F32 = jnp.float32
BF16 = jnp.bfloat16

MOBA_HEADS = 8
MOBA_HEAD_DIM = 64
MOBA_WIDTH = MOBA_HEADS * MOBA_HEAD_DIM
MOBA_BLOCK = 256
MOBA_TOPK = 3
HGRN_HEADS = 4
HGRN_DIM = 128
HGRN_WIDTH = HGRN_HEADS * HGRN_DIM
EXP_CLIP = 30.0
REL_BUCKETS = 32
REL_MAX_DIST = 2048
RMS_EPS = 1e-6
MASK_VALUE = -1e30

LANES = 128
SUBLANES = 8
VMEM_LIMIT_BYTES = 56 * 1024 * 1024

LOG2E = math.log2(math.e)
HEADS_PER_PAIR = LANES // MOBA_HEAD_DIM
MOBA_PAIRS_PER_STEP = 2
MOBA_HEADS_PER_STEP = HEADS_PER_PAIR * MOBA_PAIRS_PER_STEP
MOBA_GROUP = 2
MASK_LANES = 32
IN_PROJ_TILE = 512
VT_ROWS = MOBA_HEAD_DIM + 16

_MAX_EXACT = REL_BUCKETS // 2
_LAST_BUCKET_DIST = math.ceil(_MAX_EXACT * (REL_MAX_DIST / _MAX_EXACT) ** ((REL_BUCKETS - 1 - _MAX_EXACT) / (REL_BUCKETS - _MAX_EXACT)))
NEAR_BLOCKS = -(-(_LAST_BUCKET_DIST + MOBA_BLOCK - 1) // MOBA_BLOCK)

HGRN_CHUNK = 64
HGRN_LEVELS = (32, 16, 8, 4, 2, 1)


def _sigmoid(x):
    return 0.5 * jnp.tanh(0.5 * x) + 0.5


def _silu(x):
    return x * _sigmoid(x)


def _dot(a, b):
    return jnp.dot(a, b, preferred_element_type=F32)


def _dot_nt(a, b):
    return lax.dot_general(a, b, (((1,), (1,)), ((), ())), preferred_element_type=F32)


def _dot_tn(a, b):
    return lax.dot_general(a, b, (((0,), (0,)), ((), ())), preferred_element_type=F32)


def _bias_tiles_kernel(tab_ref, out_ref):
    h = pl.program_id(0)
    d = pl.program_id(1)
    span = 2 * MOBA_BLOCK
    lane = lax.broadcasted_iota(jnp.int32, (SUBLANES, span), 1)
    rel = d * MOBA_BLOCK + jnp.where(lane < MOBA_BLOCK, lane, lane - span)
    n = jnp.maximum(rel, 0)
    nf = jnp.maximum(n, _MAX_EXACT).astype(F32)
    large = _MAX_EXACT + (jnp.log(nf / _MAX_EXACT) / math.log(REL_MAX_DIST / _MAX_EXACT)
                          * (REL_BUCKETS - _MAX_EXACT)).astype(jnp.int32)
    large = jnp.minimum(large, REL_BUCKETS - 1)
    bucket = jnp.where(n < _MAX_EXACT, n, large)
    val = jnp.zeros((SUBLANES, span), F32)
    for b in range(REL_BUCKETS):
        val = jnp.where(bucket == b, tab_ref[b, h], val)
    val = jnp.where(rel >= 0, val * LOG2E, MASK_VALUE)
    val = jnp.where(d < NEAR_BLOCKS, val, 0.0)
    rows = jnp.broadcast_to(val[0:1], (MOBA_BLOCK, span))
    out_ref[0, 0] = pltpu.roll(rows, 0, 1, stride=1, stride_axis=0)[:, :MOBA_BLOCK]


def _bias_tiles(rel_bias):
    return pl.pallas_call(
        _bias_tiles_kernel,
        grid=(MOBA_HEADS, NEAR_BLOCKS + 1),
        in_specs=[pl.BlockSpec(memory_space=pltpu.SMEM)],
        out_specs=pl.BlockSpec((1, 1, MOBA_BLOCK, MOBA_BLOCK), lambda h, d: (h, d, 0, 0)),
        out_shape=jax.ShapeDtypeStruct((MOBA_HEADS, NEAR_BLOCKS + 1, MOBA_BLOCK, MOBA_BLOCK), F32),
        name="t5_bias_tiles",
    )(rel_bias.astype(F32))


def _split_bf16(x):
    hi = x.astype(BF16)
    lo = (x - hi.astype(F32)).astype(BF16)
    return hi, lo


def _moba_queries(blk, slot, pp, qt_pair, km_pair, tab_ref, qaug_ref):
    tq = MOBA_BLOCK
    scale = MOBA_HEAD_DIM ** -0.5 * LOG2E
    lane = lax.broadcasted_iota(jnp.int32, (MASK_LANES, LANES), 1)
    kblk = lax.broadcasted_iota(jnp.int32, (MASK_LANES, tq), 0)
    kblk_f = kblk.astype(F32)
    past = kblk < blk
    q_hi, q_lo = _split_bf16(qt_pair)
    for hh in range(HEADS_PER_PAIR):
        h = pp * HEADS_PER_PAIR + hh
        in_head = (lane >= hh * MOBA_HEAD_DIM) & (lane < (hh + 1) * MOBA_HEAD_DIM)
        k_hi, k_lo = _split_bf16(jnp.where(in_head, km_pair, 0.0))
        gate = _dot(k_hi, q_hi) + _dot(k_hi, q_lo) + _dot(k_lo, q_hi)
        gate = jnp.where(past, gate, -jnp.inf)
        sel = jnp.zeros((MASK_LANES, tq), jnp.bool_)
        for _ in range(MOBA_TOPK):
            best = jnp.max(gate, axis=0, keepdims=True)
            first = jnp.min(jnp.where(gate == best, kblk_f, float(MASK_LANES)), axis=0, keepdims=True)
            hit = kblk_f == first
            sel = sel | hit
            gate = jnp.where(hit, -jnp.inf, gate)
        sel = (sel & past) | (kblk == blk)
        far_bias = jnp.where(blk - kblk >= NEAR_BLOCKS, tab_ref[REL_BUCKETS - 1, h] * LOG2E, 0.0)
        mask_hi, mask_lo = _split_bf16(jnp.where(sel, far_bias, MASK_VALUE))
        q_rows = (qt_pair[hh * MOBA_HEAD_DIM:(hh + 1) * MOBA_HEAD_DIM] * scale).astype(BF16)
        parts = [q_rows, mask_hi, mask_lo] if hh == 0 else [mask_hi, mask_lo, q_rows]
        qaug_ref[0, h, slot] = jnp.concatenate(parts, axis=0)


def _in_proj_kernel(layer, tab_ref, x_ref, gain_ref, w_ref, qg_ref, kg_ref, lbl_ref, hmean_ref,
                    qaug_ref, k_ref, vt_ref, ag_ref, hq_ref, hk_ref, hv_ref, hg_ref,
                    bg_ref, ga_ref, gb_ref, kmean_ref):
    tile_blocks = x_ref.shape[1] // MOBA_BLOCK
    first_blk = pl.program_id(1) * tile_blocks
    x = x_ref[0]
    xn = x * lax.rsqrt(jnp.mean(x * x, axis=-1, keepdims=True) + RMS_EPS) * gain_ref[...]
    xn = xn.astype(BF16)

    def seg(start, width):
        return _dot(xn, w_ref[0, :, start:start + width])

    def head_rms(t, gain):
        ms = _dot((t * t).astype(BF16), hmean_ref[...])
        return t * lax.rsqrt(ms + RMS_EPS) * gain

    w = MOBA_WIDTH
    qn = head_rms(seg(0, w), qg_ref[...])
    kn = head_rms(seg(w, w), kg_ref[...])

    @pl.when(first_blk == 0)
    def _():
        kmean_ref[...] = jnp.zeros_like(kmean_ref)

    for j in range(tile_blocks):
        rows = slice(j * MOBA_BLOCK, (j + 1) * MOBA_BLOCK)
        kmean_ref[pl.ds(first_blk + j, 1), :] = jnp.mean(kn[rows], axis=0, keepdims=True)
    kmean = kmean_ref[...]
    gate_pieces = []
    for j in range(tile_blocks):
        qt = qn[j * MOBA_BLOCK:(j + 1) * MOBA_BLOCK].T
        for pp in range(MOBA_HEADS // HEADS_PER_PAIR):
            lanes = slice(pp * LANES, (pp + 1) * LANES)
            gate_pieces.append(functools.partial(
                _moba_queries, first_blk + j, j, pp, qt[lanes], kmean[:, lanes], tab_ref, qaug_ref))

    def next_gate_pieces(n):
        for _ in range(n):
            if gate_pieces:
                gate_pieces.pop(0)()

    lane = lax.broadcasted_iota(jnp.int32, (x.shape[0], LANES), 1)
    row = lax.broadcasted_iota(jnp.int32, (x.shape[0], LANES), 0)
    row_blk = first_blk + jnp.right_shift(row, MOBA_BLOCK.bit_length() - 1)
    block_hot = jnp.where((lane & (MASK_LANES - 1)) == row_blk, 1.0, 0.0)
    for pp in range(MOBA_HEADS // HEADS_PER_PAIR):
        kp = kn[:, pp * LANES:(pp + 1) * LANES]
        for hh, own_lanes in enumerate((lane < MOBA_HEAD_DIM, lane >= MOBA_HEAD_DIM)):
            k_head = jnp.where(own_lanes, kp, block_hot).astype(BF16)
            k_ref[0, HEADS_PER_PAIR * pp + hh] = pltpu.bitcast(k_head, jnp.uint32)
    vt = seg(2 * w, w)
    next_gate_pieces(1)
    vt = vt.T
    extra = lax.broadcasted_iota(jnp.int32, (VT_ROWS - MOBA_HEAD_DIM, x.shape[0]), 0)
    ones_row = jnp.where(extra == 0, 1.0, 0.0).astype(BF16)
    for h in range(MOBA_HEADS):
        v_head = vt[h * MOBA_HEAD_DIM:(h + 1) * MOBA_HEAD_DIM].astype(BF16)
        vt_ref[0, h, :MOBA_HEAD_DIM // 2] = pltpu.bitcast(v_head, jnp.uint32)
        vt_ref[0, h, MOBA_HEAD_DIM // 2:] = pltpu.bitcast(ones_row, jnp.uint32)
    t = seg(3 * w, w)
    next_gate_pieces(1)
    ag_ref[0] = _silu(t).astype(BF16)

    base = 4 * w
    hw = HGRN_WIDTH
    t = seg(base, hw)
    next_gate_pieces(1)
    hq_ref[0] = _silu(t)
    lbl = lbl_ref[...]
    e = jnp.exp(lbl - jnp.max(lbl, axis=0, keepdims=True))
    sm = e / jnp.sum(e, axis=0, keepdims=True)
    lb = jnp.zeros((1, hw), F32)
    for j in range(1, layer + 1):
        lb = lb + sm[j:j + 1, :]
    z = seg(base + hw, hw)
    next_gate_pieces(1)
    e = jnp.exp(-jnp.abs(z))
    pos = z >= 0.0
    inv = 1.0 / (1.0 + e)
    clipped = jnp.where(pos, e, jnp.minimum(1.0 / e, math.exp(EXP_CLIP)))
    hg_ref[0] = jnp.minimum(z, 0.0) * LOG2E + jnp.log2((1.0 + lb * clipped) * inv)
    hk_ref[0] = (1.0 - lb) * jnp.where(pos, e, 1.0) * inv
    hv_ref[0] = seg(base + 2 * hw, hw)
    next_gate_pieces(1)
    t = seg(base + 3 * hw, hw)
    next_gate_pieces(1)
    bg_ref[0] = _silu(t).astype(BF16)

    base = base + 4 * hw
    d = x.shape[-1]
    t = seg(base, d)
    next_gate_pieces(1)
    ga_ref[0] = _sigmoid(t).astype(BF16)
    t = seg(base + d, d)
    next_gate_pieces(len(gate_pieces))
    gb_ref[0] = _sigmoid(t).astype(BF16)


def _in_proj(layer, rel_bias, h, norm_gain, w_in, q_gain, k_gain, lb_logits, hmean):
    b, s, d = h.shape
    depth = w_in.shape[0]
    tm = IN_PROJ_TILE
    nt = s // tm
    tile_blocks = tm // MOBA_BLOCK
    nb = s // MOBA_BLOCK
    assert s % tm == 0 and nb <= MASK_LANES and HEADS_PER_PAIR == 2
    in_width = w_in.shape[-1]
    tok = lambda width: pl.BlockSpec((1, tm, width), lambda bi, ti: (bi, ti, 0))
    const2 = lambda shape: pl.BlockSpec(shape, lambda bi, ti: (0, 0))
    tok_shape = lambda width, dtype: jax.ShapeDtypeStruct((b, s, width), dtype)
    outs = (
        (pl.BlockSpec((1, MOBA_HEADS, tile_blocks, LANES, MOBA_BLOCK), lambda bi, ti: (bi, 0, ti, 0, 0)),
         jax.ShapeDtypeStruct((b, MOBA_HEADS, nb, LANES, MOBA_BLOCK), BF16)),
        (pl.BlockSpec((1, MOBA_HEADS, tm // 2, LANES), lambda bi, ti: (bi, 0, ti, 0)),
         jax.ShapeDtypeStruct((b, MOBA_HEADS, s // 2, LANES), jnp.uint32)),
        (pl.BlockSpec((1, MOBA_HEADS, VT_ROWS // 2, tm), lambda bi, ti: (bi, 0, 0, ti)),
         jax.ShapeDtypeStruct((b, MOBA_HEADS, VT_ROWS // 2, s), jnp.uint32)),
        (tok(MOBA_WIDTH), tok_shape(MOBA_WIDTH, BF16)),
        (tok(HGRN_WIDTH), tok_shape(HGRN_WIDTH, F32)),
        (tok(HGRN_WIDTH), tok_shape(HGRN_WIDTH, F32)),
        (tok(HGRN_WIDTH), tok_shape(HGRN_WIDTH, F32)),
        (tok(HGRN_WIDTH), tok_shape(HGRN_WIDTH, F32)),
        (tok(HGRN_WIDTH), tok_shape(HGRN_WIDTH, BF16)),
        (tok(d), tok_shape(d, BF16)),
        (tok(d), tok_shape(d, BF16)),
    )
    return pl.pallas_call(
        functools.partial(_in_proj_kernel, layer),
        grid=(b, nt),
        in_specs=[
            pl.BlockSpec(memory_space=pltpu.SMEM),
            tok(d),
            const2((1, d)),
            pl.BlockSpec((1, d, in_width), lambda bi, ti: (layer, 0, 0), pipeline_mode=pl.Buffered(1)),
            const2((1, MOBA_WIDTH)),
            const2((1, MOBA_WIDTH)),
            const2((depth, HGRN_WIDTH)),
            const2((MOBA_WIDTH, MOBA_WIDTH)),
        ],
        out_specs=[o[0] for o in outs],
        out_shape=[o[1] for o in outs],
        scratch_shapes=[pltpu.VMEM((MASK_LANES, MOBA_WIDTH), F32)],
        compiler_params=pltpu.CompilerParams(
            dimension_semantics=("arbitrary", "arbitrary"), vmem_limit_bytes=VMEM_LIMIT_BYTES),
        name=f"in_proj_l{layer}",
    )(rel_bias.astype(F32), h, norm_gain, w_in, q_gain, k_gain, lb_logits, hmean)


def _moba_kernel(qaug_ref, qnext_ref, k_ref, vt_ref, bias_ref, out_ref, s0_ref, s1_ref, max0_ref, max1_ref):
    blk = pl.program_id(2)
    tq = MOBA_BLOCK
    group_rows = MOBA_GROUP * MOBA_BLOCK

    def score_head(g, h, bufs, near, q_ref=qaug_ref, q_blk=blk):
        s_ref, max_ref = bufs
        words = pl.ds(pl.multiple_of(g * (group_rows // 2), group_rows // 2), group_rows // 2)
        k_rows = pltpu.bitcast(k_ref[0, h, words, :], BF16)
        s = _dot(k_rows, q_ref[0, h, 0])
        if near:
            parts = []
            for u in range(MOBA_GROUP):
                dist = q_blk - (g * MOBA_GROUP + u)
                tile = bias_ref[h, jnp.where((dist >= 0) & (dist < NEAR_BLOCKS), dist, NEAR_BLOCKS)]
                parts.append(s[u * MOBA_BLOCK:(u + 1) * MOBA_BLOCK] + tile)
            s = jnp.concatenate(parts, axis=0)
        s_ref[h] = s
        max_ref[h] = jnp.max(s.reshape(group_rows // SUBLANES, SUBLANES, tq), axis=0)

    def softmax_head(g, h, bufs, m, acc):
        s_ref, max_ref = bufs
        rows = pl.ds(pl.multiple_of(g * group_rows, group_rows), group_rows)
        m_new = jnp.maximum(m, jnp.max(max_ref[h], axis=0, keepdims=True))
        p = jnp.exp2(s_ref[h] - m_new).astype(BF16)
        acc = jnp.exp2(m - m_new) * acc + _dot(pltpu.bitcast(vt_ref[0, h, :, rows], BF16), p)
        return m_new, acc

    heads = range(MOBA_HEADS_PER_STEP)
    state = []
    for _ in heads:
        state += [jnp.full((1, tq), MASK_VALUE, F32), jnp.zeros((VT_ROWS, tq), F32)]
    bufs = ((s0_ref, max0_ref), (s1_ref, max1_ref))

    def group_pair(i, state, near_next):
        state = list(state)
        for parity in range(2):
            for h in heads:
                state[2 * h], state[2 * h + 1] = softmax_head(
                    2 * i + parity, h, bufs[parity], state[2 * h], state[2 * h + 1])
                score_head(2 * i + 2 + parity, h, bufs[parity], near_next)
        return state

    pair_blocks = 2 * MOBA_GROUP
    last = blk // pair_blocks
    n_far = jnp.maximum(jnp.maximum(blk - (NEAR_BLOCKS - 1), 0) // pair_blocks - 1, 0)

    @pl.when(blk == 0)
    def _():
        for parity in range(2):
            for h in heads:
                score_head(parity, h, bufs[parity], True)

    def pairs(lo, hi, state, near_next):
        def two(j, st):
            return tuple(group_pair(lo + 2 * j + 1, group_pair(lo + 2 * j, st, near_next), near_next))

        count = hi - lo
        state = lax.fori_loop(0, count // 2, two, tuple(state))
        return lax.cond(count % 2 == 1, lambda st: tuple(group_pair(hi - 1, st, near_next)),
                        lambda st: tuple(st), state)

    state = pairs(0, n_far, state, False)
    state = pairs(n_far, last, state, True)

    state = list(state)
    for parity in range(2):
        for h in heads:
            state[2 * h], state[2 * h + 1] = softmax_head(
                2 * last + parity, h, bufs[parity], state[2 * h], state[2 * h + 1])
            score_head(parity, h, bufs[parity], True, qnext_ref, blk + 1)

    accs = state[1::2]
    out_t = jnp.concatenate([a[:MOBA_HEAD_DIM] / a[MOBA_HEAD_DIM:MOBA_HEAD_DIM + 1] for a in accs], axis=0)
    out_ref[0] = out_t.T.astype(out_ref.dtype)


def _moba(qaug, k, vt, bias_tiles):
    b, _, _, s = vt.shape
    nb = s // MOBA_BLOCK
    assert nb % (2 * MOBA_GROUP) == 0
    hs = MOBA_HEADS_PER_STEP
    width = hs * MOBA_HEAD_DIM
    return pl.pallas_call(
        _moba_kernel,
        grid=(b, MOBA_HEADS // hs, nb),
        in_specs=[
            pl.BlockSpec((1, hs, 1, LANES, MOBA_BLOCK), lambda bi, pi, ti: (bi, pi, ti, 0, 0)),
            pl.BlockSpec((1, hs, 1, LANES, MOBA_BLOCK),
                         lambda bi, pi, ti: (bi, pi, jnp.minimum(ti + 1, nb - 1), 0, 0)),
            pl.BlockSpec((1, hs, s // 2, LANES), lambda bi, pi, ti: (bi, pi, 0, 0)),
            pl.BlockSpec((1, hs, VT_ROWS // 2, s), lambda bi, pi, ti: (bi, pi, 0, 0)),
            pl.BlockSpec((hs, NEAR_BLOCKS + 1, MOBA_BLOCK, MOBA_BLOCK),
                         lambda bi, pi, ti: (pi, 0, 0, 0), pipeline_mode=pl.Buffered(1)),
        ],
        out_specs=pl.BlockSpec((1, MOBA_BLOCK, width), lambda bi, pi, ti: (bi, ti, pi)),
        out_shape=jax.ShapeDtypeStruct((b, s, MOBA_WIDTH), BF16),
        scratch_shapes=[pltpu.VMEM((hs, MOBA_GROUP * MOBA_BLOCK, MOBA_BLOCK), F32),
                        pltpu.VMEM((hs, MOBA_GROUP * MOBA_BLOCK, MOBA_BLOCK), F32),
                        pltpu.VMEM((hs, SUBLANES, MOBA_BLOCK), F32),
                        pltpu.VMEM((hs, SUBLANES, MOBA_BLOCK), F32)],
        compiler_params=pltpu.CompilerParams(
            dimension_semantics=("arbitrary", "arbitrary", "arbitrary"), vmem_limit_bytes=VMEM_LIMIT_BYTES),
        name="moba_attention",
    )(qaug, qaug, k, vt, bias_tiles)


def _chunk_cumsum(g):
    groups = HGRN_CHUNK // SUBLANES
    g3 = g.reshape(groups, SUBLANES, g.shape[-1])
    sub = lax.broadcasted_iota(jnp.int32, g3.shape, 1)
    shift = 1
    while shift < SUBLANES:
        g3 = g3 + jnp.where(sub >= shift, pltpu.roll(g3, shift, axis=1), 0.0)
        shift *= 2
    rows = []
    run = None
    for i in range(groups):
        cur = g3[i] if run is None else g3[i] + run
        rows.append(cur)
        run = cur[SUBLANES - 1:SUBLANES, :]
    return jnp.concatenate(rows, axis=0)


def _hgrn_masks():
    c = HGRN_CHUNK
    row = lax.broadcasted_iota(jnp.int32, (c, 1), 0)
    rt = lax.broadcasted_iota(jnp.int32, (c, c), 0)
    cs = lax.broadcasted_iota(jnp.int32, (c, c), 1)
    levels = []
    for half in HGRN_LEVELS:
        shift = int(math.log2(2 * half))
        valid = ((rt >> shift) == (cs >> shift)) & ((rt & half) != 0) & ((cs & half) == 0)
        levels.append(((row & half) != 0, valid))
    return levels, rt == cs


def _hgrn_chunk_matmuls(q, k, v, g2, state_t, level_masks):
    c = HGRN_CHUNK
    cum = _chunk_cumsum(g2)
    last = cum[c - 1:c, :]
    v16 = v.astype(BF16)

    o = _dot_nt((q * jnp.exp2(cum)).astype(BF16), state_t.astype(BF16))
    k_end = (k * jnp.exp2(last - cum)).astype(BF16)
    new_state = state_t * jnp.exp2(last) + _dot_tn(v16, k_end)

    scores = jnp.zeros((c, c), F32)
    groups = c // SUBLANES
    width = cum.shape[-1]
    cum3 = cum.reshape(groups, SUBLANES, width)
    sub = lax.broadcasted_iota(jnp.int32, (groups, SUBLANES, width), 1)

    def row_of_group(r):
        return jnp.broadcast_to(cum3[:, r:r + 1, :], cum3.shape)

    for half, (is_q, valid) in zip(HGRN_LEVELS, level_masks):
        if half == 1:
            exponent = jnp.where(is_q, g2, 0.0)
        else:
            if half >= SUBLANES:
                ref = jnp.concatenate(
                    [jnp.broadcast_to(cum[lo + half - 1:lo + half, :], (2 * half, width))
                     for lo in range(0, c, 2 * half)], axis=0)
            elif half == 4:
                ref = row_of_group(3).reshape(c, width)
            else:
                ref = jnp.where(sub < 4, row_of_group(1), row_of_group(5)).reshape(c, width)
            exponent = jnp.where(is_q, cum - ref, ref - cum)
        x = (jnp.where(is_q, q, k) * jnp.exp2(exponent)).astype(BF16)
        scores = scores + jnp.where(valid, _dot_nt(x, x), 0.0)
    return o, new_state, scores, v16


def _hgrn_kernel(q_ref, k_ref, v_ref, g_ref, out_ref, state_ref):
    @pl.when(pl.program_id(1) == 0)
    def _():
        state_ref[...] = jnp.zeros_like(state_ref)

    n_chunks = q_ref.shape[1] // HGRN_CHUNK
    masks = _hgrn_masks()

    level_masks, on_diagonal = masks

    def chunk(ci, carry):
        rows = pl.ds(pl.multiple_of(ci * HGRN_CHUNK, HGRN_CHUNK), HGRN_CHUNK)
        partial = []
        for h in range(HGRN_HEADS):
            cols = slice(h * HGRN_DIM, (h + 1) * HGRN_DIM)
            q, k = q_ref[0, rows, cols], k_ref[0, rows, cols]
            o, new_state, scores, v16 = _hgrn_chunk_matmuls(
                q, k, v_ref[0, rows, cols], g_ref[0, rows, cols], state_ref[h], level_masks)
            state_ref[h] = new_state
            partial.append((jnp.sum(q * k, axis=-1, keepdims=True), o, scores, v16))
        for h, (self_score, o, scores, v16) in enumerate(partial):
            scores = jnp.where(on_diagonal, self_score, scores)
            out = o + _dot(scores.astype(BF16), v16)
            out_ref[0, rows, h * HGRN_DIM:(h + 1) * HGRN_DIM] = out.astype(out_ref.dtype)
        return carry

    lax.fori_loop(0, n_chunks, chunk, 0, unroll=8)


def _hgrn(hq, hk, hv, hg):
    b, s, w = hq.shape
    ts = min(s, 512)
    spec = pl.BlockSpec((1, ts, w), lambda bi, ti: (bi, ti, 0))
    return pl.pallas_call(
        _hgrn_kernel,
        grid=(b, s // ts),
        in_specs=[spec, spec, spec, spec],
        out_specs=spec,
        out_shape=jax.ShapeDtypeStruct((b, s, w), BF16),
        scratch_shapes=[pltpu.VMEM((HGRN_HEADS, HGRN_DIM, HGRN_DIM), F32)],
        compiler_params=pltpu.CompilerParams(
            dimension_semantics=("arbitrary", "arbitrary"), vmem_limit_bytes=VMEM_LIMIT_BYTES),
        name="hgrn2_recurrence",
    )(hq, hk, hv, hg)


def _merge_kernel(h_ref, ya_ref, ag_ref, yb_ref, bg_ref, ga_ref, gb_ref, p_ref, og_ref,
                  wa_ref, wb_ref, wo_ref, wp_ref, wg_ref, out_ref):
    ya = _dot(ya_ref[0] * ag_ref[0], wa_ref[0])

    yb = yb_ref[0].astype(F32)
    normed = []
    for h in range(HGRN_HEADS):
        t = yb[:, h * HGRN_DIM:(h + 1) * HGRN_DIM]
        normed.append(t * lax.rsqrt(jnp.mean(t * t, axis=-1, keepdims=True) + RMS_EPS))
    yb = jnp.concatenate(normed, axis=-1) * og_ref[...]
    yb = _dot((yb * bg_ref[0]).astype(BF16), wb_ref[0])

    merged = ga_ref[0] * ya + gb_ref[0] * yb
    h = h_ref[0] + _dot(merged.astype(BF16), wo_ref[0])
    ple = _dot(p_ref[0, 0].astype(BF16), wp_ref[0]) * _sigmoid(_dot(h.astype(BF16), wg_ref[0]))
    out_ref[0] = h + ple


def _merge(layer, h, ya, ag, yb, bg, ga, gb, p, out_gain, w_up_a, w_up_b, w_out, w_ple, w_ple_gate):
    b, s, d = h.shape
    tm = min(s, 512)
    tok = lambda width: pl.BlockSpec((1, tm, width), lambda bi, ti: (bi, ti, 0))
    wspec = lambda w: pl.BlockSpec((1,) + w.shape[1:], lambda bi, ti: (layer, 0, 0), pipeline_mode=pl.Buffered(1))
    return pl.pallas_call(
        _merge_kernel,
        grid=(b, s // tm),
        in_specs=[
            tok(d), tok(MOBA_WIDTH), tok(MOBA_WIDTH), tok(HGRN_WIDTH), tok(HGRN_WIDTH), tok(d), tok(d),
            pl.BlockSpec((1, 1, tm, p.shape[-1]), lambda bi, ti: (layer, bi, ti, 0)),
            pl.BlockSpec((1, HGRN_WIDTH), lambda bi, ti: (0, 0)),
            wspec(w_up_a), wspec(w_up_b), wspec(w_out), wspec(w_ple), wspec(w_ple_gate),
        ],
        out_specs=tok(d),
        out_shape=jax.ShapeDtypeStruct((b, s, d), F32),
        compiler_params=pltpu.CompilerParams(
            dimension_semantics=("arbitrary", "arbitrary"), vmem_limit_bytes=VMEM_LIMIT_BYTES),
        name=f"merge_l{layer}",
    )(h, ya, ag, yb, bg, ga, gb, p, out_gain, w_up_a, w_up_b, w_out, w_ple, w_ple_gate)


def kernel(x, p, norm_gain, w_in, q_norm_gain, k_norm_gain, rel_bias, hgrn_lb_logits, hgrn_out_gain,
           w_up_a, w_up_b, w_out, w_ple, w_ple_gate):
    b, s, d = x.shape
    depth = w_in.shape[0]
    assert s % MOBA_BLOCK == 0 and s % HGRN_CHUNK == 0

    w_in, w_up_a, w_up_b, w_out, w_ple, w_ple_gate = (
        w.astype(BF16) for w in (w_in, w_up_a, w_up_b, w_out, w_ple, w_ple_gate))
    head_id = jnp.arange(MOBA_WIDTH) // MOBA_HEAD_DIM
    hmean = ((head_id[:, None] == head_id[None, :]).astype(F32) / MOBA_HEAD_DIM).astype(BF16)
    tile = lambda g: jnp.tile(g.astype(F32), MOBA_HEADS)[None, :]

    bias_tiles = _bias_tiles(rel_bias)
    h = x
    for i in range(depth):
        qaug, k, vt, ag, hq, hk, hv, hg, bg, ga, gb = _in_proj(
            i, rel_bias, h, norm_gain[i][None, :].astype(F32), w_in, tile(q_norm_gain[i]), tile(k_norm_gain[i]),
            hgrn_lb_logits.astype(F32), hmean)
        ya = _moba(qaug, k, vt, bias_tiles)
        yb = _hgrn(hq, hk, hv, hg)
        h = _merge(i, h, ya, ag, yb, bg, ga, gb, p, hgrn_out_gain[i][None, :].astype(F32),
                   w_up_a, w_up_b, w_out, w_ple, w_ple_gate)
    return h
```

```python
import functools
import math

import jax
import jax.numpy as jnp
from jax import lax
from jax.experimental import pallas as pl
from jax.experimental.pallas import tpu as pltpu

F32 = jnp.float32
BF16 = jnp.bfloat16

MOBA_HEADS = 8
MOBA_HEAD_DIM = 64
MOBA_WIDTH = MOBA_HEADS * MOBA_HEAD_DIM
MOBA_BLOCK = 256
MOBA_TOPK = 3
HGRN_HEADS = 4
HGRN_DIM = 128
HGRN_WIDTH = HGRN_HEADS * HGRN_DIM
EXP_CLIP = 30.0
REL_BUCKETS = 32
REL_MAX_DIST = 2048
RMS_EPS = 1e-6
MASK_VALUE = -1e30

LANES = 128
SUBLANES = 8
VMEM_LIMIT_BYTES = 56 * 1024 * 1024

LOG2E = math.log2(math.e)
HEADS_PER_PAIR = LANES // MOBA_HEAD_DIM
MOBA_PAIRS_PER_STEP = 2
MOBA_HEADS_PER_STEP = HEADS_PER_PAIR * MOBA_PAIRS_PER_STEP
MOBA_GROUP = 2
MASK_LANES = 32
IN_PROJ_TILE = 512
VT_ROWS = MOBA_HEAD_DIM + 16

_MAX_EXACT = REL_BUCKETS // 2
_LAST_BUCKET_DIST = math.ceil(_MAX_EXACT * (REL_MAX_DIST / _MAX_EXACT) ** ((REL_BUCKETS - 1 - _MAX_EXACT) / (REL_BUCKETS - _MAX_EXACT)))
NEAR_BLOCKS = -(-(_LAST_BUCKET_DIST + MOBA_BLOCK - 1) // MOBA_BLOCK)

HGRN_CHUNK = 64
HGRN_LEVELS = (32, 16, 8, 4, 2, 1)


def _sigmoid(x):
    return 0.5 * jnp.tanh(0.5 * x) + 0.5


def _silu(x):
    return x * _sigmoid(x)


def _dot(a, b):
    return jnp.dot(a, b, preferred_element_type=F32)


def _dot_nt(a, b):
    return lax.dot_general(a, b, (((1,), (1,)), ((), ())), preferred_element_type=F32)


def _dot_tn(a, b):
    return lax.dot_general(a, b, (((0,), (0,)), ((), ())), preferred_element_type=F32)


def _bias_tiles_kernel(tab_ref, out_ref):
    h = pl.program_id(0)
    d = pl.program_id(1)
    span = 2 * MOBA_BLOCK
    lane = lax.broadcasted_iota(jnp.int32, (SUBLANES, span), 1)
    rel = d * MOBA_BLOCK + jnp.where(lane < MOBA_BLOCK, lane, lane - span)
    n = jnp.maximum(rel, 0)
    nf = jnp.maximum(n, _MAX_EXACT).astype(F32)
    large = _MAX_EXACT + (jnp.log(nf / _MAX_EXACT) / math.log(REL_MAX_DIST / _MAX_EXACT)
                          * (REL_BUCKETS - _MAX_EXACT)).astype(jnp.int32)
    large = jnp.minimum(large, REL_BUCKETS - 1)
    bucket = jnp.where(n < _MAX_EXACT, n, large)
    val = jnp.zeros((SUBLANES, span), F32)
    for b in range(REL_BUCKETS):
        val = jnp.where(bucket == b, tab_ref[b, h], val)
    val = jnp.where(rel >= 0, val * LOG2E, MASK_VALUE)
    val = jnp.where(d < NEAR_BLOCKS, val, 0.0)
    rows = jnp.broadcast_to(val[0:1], (MOBA_BLOCK, span))
    out_ref[0, 0] = pltpu.roll(rows, 0, 1, stride=1, stride_axis=0)[:, :MOBA_BLOCK]


def _bias_tiles(rel_bias):
    return pl.pallas_call(
        _bias_tiles_kernel,
        grid=(MOBA_HEADS, NEAR_BLOCKS + 1),
        in_specs=[pl.BlockSpec(memory_space=pltpu.SMEM)],
        out_specs=pl.BlockSpec((1, 1, MOBA_BLOCK, MOBA_BLOCK), lambda h, d: (h, d, 0, 0)),
        out_shape=jax.ShapeDtypeStruct((MOBA_HEADS, NEAR_BLOCKS + 1, MOBA_BLOCK, MOBA_BLOCK), F32),
        name="t5_bias_tiles",
    )(rel_bias.astype(F32))


def _split_bf16(x):
    hi = x.astype(BF16)
    lo = (x - hi.astype(F32)).astype(BF16)
    return hi, lo


def _moba_queries(blk, slot, pp, qt_pair, km_pair, tab_ref, qaug_ref):
    tq = MOBA_BLOCK
    scale = MOBA_HEAD_DIM ** -0.5 * LOG2E
    lane = lax.broadcasted_iota(jnp.int32, (MASK_LANES, LANES), 1)
    kblk = lax.broadcasted_iota(jnp.int32, (MASK_LANES, tq), 0)
    kblk_f = kblk.astype(F32)
    past = kblk < blk
    q_hi, q_lo = _split_bf16(qt_pair)
    for hh in range(HEADS_PER_PAIR):
        h = pp * HEADS_PER_PAIR + hh
        in_head = (lane >= hh * MOBA_HEAD_DIM) & (lane < (hh + 1) * MOBA_HEAD_DIM)
        k_hi, k_lo = _split_bf16(jnp.where(in_head, km_pair, 0.0))
        gate = _dot(k_hi, q_hi) + _dot(k_hi, q_lo) + _dot(k_lo, q_hi)
        gate = jnp.where(past, gate, -jnp.inf)
        sel = jnp.zeros((MASK_LANES, tq), jnp.bool_)
        for _ in range(MOBA_TOPK):
            best = jnp.max(gate, axis=0, keepdims=True)
            first = jnp.min(jnp.where(gate == best, kblk_f, float(MASK_LANES)), axis=0, keepdims=True)
            hit = kblk_f == first
            sel = sel | hit
            gate = jnp.where(hit, -jnp.inf, gate)
        sel = (sel & past) | (kblk == blk)
        far_bias = jnp.where(blk - kblk >= NEAR_BLOCKS, tab_ref[REL_BUCKETS - 1, h] * LOG2E, 0.0)
        mask_hi, mask_lo = _split_bf16(jnp.where(sel, far_bias, MASK_VALUE))
        q_rows = (qt_pair[hh * MOBA_HEAD_DIM:(hh + 1) * MOBA_HEAD_DIM] * scale).astype(BF16)
        parts = [q_rows, mask_hi, mask_lo] if hh == 0 else [mask_hi, mask_lo, q_rows]
        qaug_ref[0, h, slot] = jnp.concatenate(parts, axis=0)


def _in_proj_kernel(layer, tab_ref, x_ref, gain_ref, w_ref, qg_ref, kg_ref, lbl_ref, hmean_ref,
                    qaug_ref, k_ref, vt_ref, ag_ref, hq_ref, hk_ref, hv_ref, hg_ref,
                    bg_ref, ga_ref, gb_ref, kmean_ref):
    tile_blocks = x_ref.shape[1] // MOBA_BLOCK
    first_blk = pl.program_id(1) * tile_blocks
    x = x_ref[0]
    xn = x * lax.rsqrt(jnp.mean(x * x, axis=-1, keepdims=True) + RMS_EPS) * gain_ref[...]
    xn = xn.astype(BF16)

    def seg(start, width):
        return _dot(xn, w_ref[0, :, start:start + width])

    def head_rms(t, gain):
        ms = _dot((t * t).astype(BF16), hmean_ref[...])
        return t * lax.rsqrt(ms + RMS_EPS) * gain

    w = MOBA_WIDTH
    qn = head_rms(seg(0, w), qg_ref[...])
    kn = head_rms(seg(w, w), kg_ref[...])

    @pl.when(first_blk == 0)
    def _():
        kmean_ref[...] = jnp.zeros_like(kmean_ref)

    for j in range(tile_blocks):
        rows = slice(j * MOBA_BLOCK, (j + 1) * MOBA_BLOCK)
        qt = qn[rows].T
        kmean = kmean_ref[...]
        for pp in range(MOBA_HEADS // HEADS_PER_PAIR):
            lanes = slice(pp * LANES, (pp + 1) * LANES)
            _moba_queries(first_blk + j, j, pp, qt[lanes], kmean[:, lanes], tab_ref, qaug_ref)
        kmean_ref[pl.ds(first_blk + j, 1), :] = jnp.mean(kn[rows], axis=0, keepdims=True)

    lane = lax.broadcasted_iota(jnp.int32, (x.shape[0], LANES), 1)
    row = lax.broadcasted_iota(jnp.int32, (x.shape[0], LANES), 0)
    row_blk = first_blk + jnp.right_shift(row, MOBA_BLOCK.bit_length() - 1)
    block_hot = jnp.where((lane & (MASK_LANES - 1)) == row_blk, 1.0, 0.0)
    for pp in range(MOBA_HEADS // HEADS_PER_PAIR):
        kp = kn[:, pp * LANES:(pp + 1) * LANES]
        for hh, own_lanes in enumerate((lane < MOBA_HEAD_DIM, lane >= MOBA_HEAD_DIM)):
            k_head = jnp.where(own_lanes, kp, block_hot).astype(BF16)
            k_ref[0, HEADS_PER_PAIR * pp + hh] = pltpu.bitcast(k_head, jnp.uint32)
    vt = seg(2 * w, w).T
    extra = lax.broadcasted_iota(jnp.int32, (VT_ROWS - MOBA_HEAD_DIM, x.shape[0]), 0)
    ones_row = jnp.where(extra == 0, 1.0, 0.0).astype(BF16)
    for h in range(MOBA_HEADS):
        v_head = vt[h * MOBA_HEAD_DIM:(h + 1) * MOBA_HEAD_DIM].astype(BF16)
        vt_ref[0, h, :MOBA_HEAD_DIM // 2] = pltpu.bitcast(v_head, jnp.uint32)
        vt_ref[0, h, MOBA_HEAD_DIM // 2:] = pltpu.bitcast(ones_row, jnp.uint32)
    ag_ref[0] = _silu(seg(3 * w, w)).astype(BF16)

    base = 4 * w
    hw = HGRN_WIDTH
    hq_ref[0] = _silu(seg(base, hw))
    lbl = lbl_ref[...]
    e = jnp.exp(lbl - jnp.max(lbl, axis=0, keepdims=True))
    sm = e / jnp.sum(e, axis=0, keepdims=True)
    lb = jnp.zeros((1, hw), F32)
    for j in range(1, layer + 1):
        lb = lb + sm[j:j + 1, :]
    z = seg(base + hw, hw)
    e = jnp.exp(-jnp.abs(z))
    pos = z >= 0.0
    inv = 1.0 / (1.0 + e)
    clipped = jnp.where(pos, e, jnp.minimum(1.0 / e, math.exp(EXP_CLIP)))
    hg_ref[0] = jnp.minimum(z, 0.0) * LOG2E + jnp.log2((1.0 + lb * clipped) * inv)
    hk_ref[0] = (1.0 - lb) * jnp.where(pos, e, 1.0) * inv
    hv_ref[0] = seg(base + 2 * hw, hw)
    bg_ref[0] = _silu(seg(base + 3 * hw, hw)).astype(BF16)

    base = base + 4 * hw
    d = x.shape[-1]
    ga_ref[0] = _sigmoid(seg(base, d)).astype(BF16)
    gb_ref[0] = _sigmoid(seg(base + d, d)).astype(BF16)


def _in_proj(layer, rel_bias, h, norm_gain, w_in, q_gain, k_gain, lb_logits, hmean):
    b, s, d = h.shape
    depth = w_in.shape[0]
    tm = IN_PROJ_TILE
    nt = s // tm
    tile_blocks = tm // MOBA_BLOCK
    nb = s // MOBA_BLOCK
    assert s % tm == 0 and nb <= MASK_LANES and HEADS_PER_PAIR == 2
    in_width = w_in.shape[-1]
    tok = lambda width: pl.BlockSpec((1, tm, width), lambda bi, ti: (bi, ti, 0))
    const2 = lambda shape: pl.BlockSpec(shape, lambda bi, ti: (0, 0))
    tok_shape = lambda width, dtype: jax.ShapeDtypeStruct((b, s, width), dtype)
    outs = (
        (pl.BlockSpec((1, MOBA_HEADS, tile_blocks, LANES, MOBA_BLOCK), lambda bi, ti: (bi, 0, ti, 0, 0)),
         jax.ShapeDtypeStruct((b, MOBA_HEADS, nb, LANES, MOBA_BLOCK), BF16)),
        (pl.BlockSpec((1, MOBA_HEADS, tm // 2, LANES), lambda bi, ti: (bi, 0, ti, 0)),
         jax.ShapeDtypeStruct((b, MOBA_HEADS, s // 2, LANES), jnp.uint32)),
        (pl.BlockSpec((1, MOBA_HEADS, VT_ROWS // 2, tm), lambda bi, ti: (bi, 0, 0, ti)),
         jax.ShapeDtypeStruct((b, MOBA_HEADS, VT_ROWS // 2, s), jnp.uint32)),
        (tok(MOBA_WIDTH), tok_shape(MOBA_WIDTH, BF16)),
        (tok(HGRN_WIDTH), tok_shape(HGRN_WIDTH, F32)),
        (tok(HGRN_WIDTH), tok_shape(HGRN_WIDTH, F32)),
        (tok(HGRN_WIDTH), tok_shape(HGRN_WIDTH, F32)),
        (tok(HGRN_WIDTH), tok_shape(HGRN_WIDTH, F32)),
        (tok(HGRN_WIDTH), tok_shape(HGRN_WIDTH, BF16)),
        (tok(d), tok_shape(d, BF16)),
        (tok(d), tok_shape(d, BF16)),
    )
    return pl.pallas_call(
        functools.partial(_in_proj_kernel, layer),
        grid=(b, nt),
        in_specs=[
            pl.BlockSpec(memory_space=pltpu.SMEM),
            tok(d),
            const2((1, d)),
            pl.BlockSpec((1, d, in_width), lambda bi, ti: (layer, 0, 0), pipeline_mode=pl.Buffered(1)),
            const2((1, MOBA_WIDTH)),
            const2((1, MOBA_WIDTH)),
            const2((depth, HGRN_WIDTH)),
            const2((MOBA_WIDTH, MOBA_WIDTH)),
        ],
        out_specs=[o[0] for o in outs],
        out_shape=[o[1] for o in outs],
        scratch_shapes=[pltpu.VMEM((MASK_LANES, MOBA_WIDTH), F32)],
        compiler_params=pltpu.CompilerParams(
            dimension_semantics=("arbitrary", "arbitrary"), vmem_limit_bytes=VMEM_LIMIT_BYTES),
        name=f"in_proj_l{layer}",
    )(rel_bias.astype(F32), h, norm_gain, w_in, q_gain, k_gain, lb_logits, hmean)


def _moba_kernel(qaug_ref, qnext_ref, k_ref, vt_ref, bias_ref, out_ref, s0_ref, s1_ref, max0_ref, max1_ref):
    blk = pl.program_id(2)
    tq = MOBA_BLOCK
    group_rows = MOBA_GROUP * MOBA_BLOCK

    def score_head(g, h, bufs, near, q_ref=qaug_ref, q_blk=blk):
        s_ref, max_ref = bufs
        words = pl.ds(pl.multiple_of(g * (group_rows // 2), group_rows // 2), group_rows // 2)
        k_rows = pltpu.bitcast(k_ref[0, h, words, :], BF16)
        s = _dot(k_rows, q_ref[0, h, 0])
        if near:
            parts = []
            for u in range(MOBA_GROUP):
                dist = q_blk - (g * MOBA_GROUP + u)
                tile = bias_ref[h, jnp.where((dist >= 0) & (dist < NEAR_BLOCKS), dist, NEAR_BLOCKS)]
                parts.append(s[u * MOBA_BLOCK:(u + 1) * MOBA_BLOCK] + tile)
            s = jnp.concatenate(parts, axis=0)
        s_ref[h] = s
        max_ref[h] = jnp.max(s.reshape(group_rows // SUBLANES, SUBLANES, tq), axis=0)

    def softmax_head(g, h, bufs, m, acc):
        s_ref, max_ref = bufs
        rows = pl.ds(pl.multiple_of(g * group_rows, group_rows), group_rows)
        m_new = jnp.maximum(m, jnp.max(max_ref[h], axis=0, keepdims=True))
        p = jnp.exp2(s_ref[h] - m_new).astype(BF16)
        acc = jnp.exp2(m - m_new) * acc + _dot(pltpu.bitcast(vt_ref[0, h, :, rows], BF16), p)
        return m_new, acc

    heads = range(MOBA_HEADS_PER_STEP)
    state = []
    for _ in heads:
        state += [jnp.full((1, tq), MASK_VALUE, F32), jnp.zeros((VT_ROWS, tq), F32)]
    bufs = ((s0_ref, max0_ref), (s1_ref, max1_ref))

    def group_pair(i, state, near_next):
        state = list(state)
        for parity in range(2):
            for h in heads:
                state[2 * h], state[2 * h + 1] = softmax_head(
                    2 * i + parity, h, bufs[parity], state[2 * h], state[2 * h + 1])
                score_head(2 * i + 2 + parity, h, bufs[parity], near_next)
        return state

    pair_blocks = 2 * MOBA_GROUP
    last = blk // pair_blocks
    n_far = jnp.maximum(jnp.maximum(blk - (NEAR_BLOCKS - 1), 0) // pair_blocks - 1, 0)

    @pl.when(blk == 0)
    def _():
        for parity in range(2):
            for h in heads:
                score_head(parity, h, bufs[parity], True)

    def pairs(lo, hi, state, near_next):
        def two(j, st):
            return tuple(group_pair(lo + 2 * j + 1, group_pair(lo + 2 * j, st, near_next), near_next))

        count = hi - lo
        state = lax.fori_loop(0, count // 2, two, tuple(state))
        return lax.cond(count % 2 == 1, lambda st: tuple(group_pair(hi - 1, st, near_next)),
                        lambda st: tuple(st), state)

    state = pairs(0, n_far, state, False)
    state = pairs(n_far, last, state, True)

    state = list(state)
    for parity in range(2):
        for h in heads:
            state[2 * h], state[2 * h + 1] = softmax_head(
                2 * last + parity, h, bufs[parity], state[2 * h], state[2 * h + 1])
            score_head(parity, h, bufs[parity], True, qnext_ref, blk + 1)

    accs = state[1::2]
    out_t = jnp.concatenate([a[:MOBA_HEAD_DIM] / a[MOBA_HEAD_DIM:MOBA_HEAD_DIM + 1] for a in accs], axis=0)
    out_ref[0] = out_t.T.astype(out_ref.dtype)


def _moba(qaug, k, vt, bias_tiles):
    b, _, _, s = vt.shape
    nb = s // MOBA_BLOCK
    assert nb % (2 * MOBA_GROUP) == 0
    hs = MOBA_HEADS_PER_STEP
    width = hs * MOBA_HEAD_DIM
    return pl.pallas_call(
        _moba_kernel,
        grid=(b, MOBA_HEADS // hs, nb),
        in_specs=[
            pl.BlockSpec((1, hs, 1, LANES, MOBA_BLOCK), lambda bi, pi, ti: (bi, pi, ti, 0, 0)),
            pl.BlockSpec((1, hs, 1, LANES, MOBA_BLOCK),
                         lambda bi, pi, ti: (bi, pi, jnp.minimum(ti + 1, nb - 1), 0, 0)),
            pl.BlockSpec((1, hs, s // 2, LANES), lambda bi, pi, ti: (bi, pi, 0, 0)),
            pl.BlockSpec((1, hs, VT_ROWS // 2, s), lambda bi, pi, ti: (bi, pi, 0, 0)),
            pl.BlockSpec((hs, NEAR_BLOCKS + 1, MOBA_BLOCK, MOBA_BLOCK),
                         lambda bi, pi, ti: (pi, 0, 0, 0), pipeline_mode=pl.Buffered(1)),
        ],
        out_specs=pl.BlockSpec((1, MOBA_BLOCK, width), lambda bi, pi, ti: (bi, ti, pi)),
        out_shape=jax.ShapeDtypeStruct((b, s, MOBA_WIDTH), BF16),
        scratch_shapes=[pltpu.VMEM((hs, MOBA_GROUP * MOBA_BLOCK, MOBA_BLOCK), F32),
                        pltpu.VMEM((hs, MOBA_GROUP * MOBA_BLOCK, MOBA_BLOCK), F32),
                        pltpu.VMEM((hs, SUBLANES, MOBA_BLOCK), F32),
                        pltpu.VMEM((hs, SUBLANES, MOBA_BLOCK), F32)],
        compiler_params=pltpu.CompilerParams(
            dimension_semantics=("arbitrary", "arbitrary", "arbitrary"), vmem_limit_bytes=VMEM_LIMIT_BYTES),
        name="moba_attention",
    )(qaug, qaug, k, vt, bias_tiles)


def _chunk_cumsum(g):
    groups = HGRN_CHUNK // SUBLANES
    g3 = g.reshape(groups, SUBLANES, g.shape[-1])
    sub = lax.broadcasted_iota(jnp.int32, g3.shape, 1)
    shift = 1
    while shift < SUBLANES:
        g3 = g3 + jnp.where(sub >= shift, pltpu.roll(g3, shift, axis=1), 0.0)
        shift *= 2
    rows = []
    run = None
    for i in range(groups):
        cur = g3[i] if run is None else g3[i] + run
        rows.append(cur)
        run = cur[SUBLANES - 1:SUBLANES, :]
    return jnp.concatenate(rows, axis=0)


def _hgrn_masks():
    c = HGRN_CHUNK
    row = lax.broadcasted_iota(jnp.int32, (c, 1), 0)
    rt = lax.broadcasted_iota(jnp.int32, (c, c), 0)
    cs = lax.broadcasted_iota(jnp.int32, (c, c), 1)
    levels = []
    for half in HGRN_LEVELS:
        shift = int(math.log2(2 * half))
        valid = ((rt >> shift) == (cs >> shift)) & ((rt & half) != 0) & ((cs & half) == 0)
        levels.append(((row & half) != 0, valid))
    return levels, rt == cs


def _hgrn_chunk_matmuls(q, k, v, g2, state_t, level_masks):
    c = HGRN_CHUNK
    cum = _chunk_cumsum(g2)
    last = cum[c - 1:c, :]
    v16 = v.astype(BF16)

    o = _dot_nt((q * jnp.exp2(cum)).astype(BF16), state_t.astype(BF16))
    k_end = (k * jnp.exp2(last - cum)).astype(BF16)
    new_state = state_t * jnp.exp2(last) + _dot_tn(v16, k_end)

    scores = jnp.zeros((c, c), F32)
    groups = c // SUBLANES
    width = cum.shape[-1]
    cum3 = cum.reshape(groups, SUBLANES, width)
    sub = lax.broadcasted_iota(jnp.int32, (groups, SUBLANES, width), 1)

    def row_of_group(r):
        return jnp.broadcast_to(cum3[:, r:r + 1, :], cum3.shape)

    for half, (is_q, valid) in zip(HGRN_LEVELS, level_masks):
        if half == 1:
            exponent = jnp.where(is_q, g2, 0.0)
        else:
            if half >= SUBLANES:
                ref = jnp.concatenate(
                    [jnp.broadcast_to(cum[lo + half - 1:lo + half, :], (2 * half, width))
                     for lo in range(0, c, 2 * half)], axis=0)
            elif half == 4:
                ref = row_of_group(3).reshape(c, width)
            else:
                ref = jnp.where(sub < 4, row_of_group(1), row_of_group(5)).reshape(c, width)
            exponent = jnp.where(is_q, cum - ref, ref - cum)
        x = (jnp.where(is_q, q, k) * jnp.exp2(exponent)).astype(BF16)
        scores = scores + jnp.where(valid, _dot_nt(x, x), 0.0)
    return o, new_state, scores, v16


def _hgrn_kernel(q_ref, k_ref, v_ref, g_ref, out_ref, state_ref):
    @pl.when(pl.program_id(1) == 0)
    def _():
        state_ref[...] = jnp.zeros_like(state_ref)

    n_chunks = q_ref.shape[1] // HGRN_CHUNK
    masks = _hgrn_masks()

    level_masks, on_diagonal = masks

    def chunk(ci, carry):
        rows = pl.ds(pl.multiple_of(ci * HGRN_CHUNK, HGRN_CHUNK), HGRN_CHUNK)
        partial = []
        for h in range(HGRN_HEADS):
            cols = slice(h * HGRN_DIM, (h + 1) * HGRN_DIM)
            q, k = q_ref[0, rows, cols], k_ref[0, rows, cols]
            o, new_state, scores, v16 = _hgrn_chunk_matmuls(
                q, k, v_ref[0, rows, cols], g_ref[0, rows, cols], state_ref[h], level_masks)
            state_ref[h] = new_state
            partial.append((jnp.sum(q * k, axis=-1, keepdims=True), o, scores, v16))
        for h, (self_score, o, scores, v16) in enumerate(partial):
            scores = jnp.where(on_diagonal, self_score, scores)
            out = o + _dot(scores.astype(BF16), v16)
            out_ref[0, rows, h * HGRN_DIM:(h + 1) * HGRN_DIM] = out.astype(out_ref.dtype)
        return carry

    lax.fori_loop(0, n_chunks, chunk, 0, unroll=8)


def _hgrn(hq, hk, hv, hg):
    b, s, w = hq.shape
    ts = min(s, 512)
    spec = pl.BlockSpec((1, ts, w), lambda bi, ti: (bi, ti, 0))
    return pl.pallas_call(
        _hgrn_kernel,
        grid=(b, s // ts),
        in_specs=[spec, spec, spec, spec],
        out_specs=spec,
        out_shape=jax.ShapeDtypeStruct((b, s, w), BF16),
        scratch_shapes=[pltpu.VMEM((HGRN_HEADS, HGRN_DIM, HGRN_DIM), F32)],
        compiler_params=pltpu.CompilerParams(
            dimension_semantics=("arbitrary", "arbitrary"), vmem_limit_bytes=VMEM_LIMIT_BYTES),
        name="hgrn2_recurrence",
    )(hq, hk, hv, hg)


def _merge_kernel(h_ref, ya_ref, ag_ref, yb_ref, bg_ref, ga_ref, gb_ref, p_ref, og_ref,
                  wa_ref, wb_ref, wo_ref, wp_ref, wg_ref, out_ref):
    ya = _dot(ya_ref[0] * ag_ref[0], wa_ref[0])

    yb = yb_ref[0].astype(F32)
    normed = []
    for h in range(HGRN_HEADS):
        t = yb[:, h * HGRN_DIM:(h + 1) * HGRN_DIM]
        normed.append(t * lax.rsqrt(jnp.mean(t * t, axis=-1, keepdims=True) + RMS_EPS))
    yb = jnp.concatenate(normed, axis=-1) * og_ref[...]
    yb = _dot((yb * bg_ref[0]).astype(BF16), wb_ref[0])

    merged = ga_ref[0] * ya + gb_ref[0] * yb
    h = h_ref[0] + _dot(merged.astype(BF16), wo_ref[0])
    ple = _dot(p_ref[0, 0].astype(BF16), wp_ref[0]) * _sigmoid(_dot(h.astype(BF16), wg_ref[0]))
    out_ref[0] = h + ple


def _merge(layer, h, ya, ag, yb, bg, ga, gb, p, out_gain, w_up_a, w_up_b, w_out, w_ple, w_ple_gate):
    b, s, d = h.shape
    tm = min(s, 512)
    tok = lambda width: pl.BlockSpec((1, tm, width), lambda bi, ti: (bi, ti, 0))
    wspec = lambda w: pl.BlockSpec((1,) + w.shape[1:], lambda bi, ti: (layer, 0, 0), pipeline_mode=pl.Buffered(1))
    return pl.pallas_call(
        _merge_kernel,
        grid=(b, s // tm),
        in_specs=[
            tok(d), tok(MOBA_WIDTH), tok(MOBA_WIDTH), tok(HGRN_WIDTH), tok(HGRN_WIDTH), tok(d), tok(d),
            pl.BlockSpec((1, 1, tm, p.shape[-1]), lambda bi, ti: (layer, bi, ti, 0)),
            pl.BlockSpec((1, HGRN_WIDTH), lambda bi, ti: (0, 0)),
            wspec(w_up_a), wspec(w_up_b), wspec(w_out), wspec(w_ple), wspec(w_ple_gate),
        ],
        out_specs=tok(d),
        out_shape=jax.ShapeDtypeStruct((b, s, d), F32),
        compiler_params=pltpu.CompilerParams(
            dimension_semantics=("arbitrary", "arbitrary"), vmem_limit_bytes=VMEM_LIMIT_BYTES),
        name=f"merge_l{layer}",
    )(h, ya, ag, yb, bg, ga, gb, p, out_gain, w_up_a, w_up_b, w_out, w_ple, w_ple_gate)


def kernel(x, p, norm_gain, w_in, q_norm_gain, k_norm_gain, rel_bias, hgrn_lb_logits, hgrn_out_gain,
           w_up_a, w_up_b, w_out, w_ple, w_ple_gate):
    b, s, d = x.shape
    depth = w_in.shape[0]
    assert s % MOBA_BLOCK == 0 and s % HGRN_CHUNK == 0

    w_in, w_up_a, w_up_b, w_out, w_ple, w_ple_gate = (
        w.astype(BF16) for w in (w_in, w_up_a, w_up_b, w_out, w_ple, w_ple_gate))
    head_id = jnp.arange(MOBA_WIDTH) // MOBA_HEAD_DIM
    hmean = ((head_id[:, None] == head_id[None, :]).astype(F32) / MOBA_HEAD_DIM).astype(BF16)
    tile = lambda g: jnp.tile(g.astype(F32), MOBA_HEADS)[None, :]

    bias_tiles = _bias_tiles(rel_bias)
    h = x
    for i in range(depth):
        qaug, k, vt, ag, hq, hk, hv, hg, bg, ga, gb = _in_proj(
            i, rel_bias, h, norm_gain[i][None, :].astype(F32), w_in, tile(q_norm_gain[i]), tile(k_norm_gain[i]),
            hgrn_lb_logits.astype(F32), hmean)
        ya = _moba(qaug, k, vt, bias_tiles)
        yb = _hgrn(hq, hk, hv, hg)
        h = _merge(i, h, ya, ag, yb, bg, ga, gb, p, hgrn_out_gain[i][None, :].astype(F32),
                   w_up_a, w_up_b, w_out, w_ple, w_ple_gate)
    return h
```

```python
import functools
import math

import jax
import jax.numpy as jnp
from jax import lax
from jax.experimental import pallas as pl
from jax.experimental.pallas import tpu as pltpu

F32 = jnp.float32
BF16 = jnp.bfloat16

MOBA_HEADS = 8
MOBA_HEAD_DIM = 64
MOBA_WIDTH = MOBA_HEADS * MOBA_HEAD_DIM
MOBA_BLOCK = 256
MOBA_TOPK = 3
HGRN_HEADS = 4
HGRN_DIM = 128
HGRN_WIDTH = HGRN_HEADS * HGRN_DIM
EXP_CLIP = 30.0
REL_BUCKETS = 32
REL_MAX_DIST = 2048
RMS_EPS = 1e-6
MASK_VALUE = -1e30

LANES = 128
SUBLANES = 8
VMEM_LIMIT_BYTES = 56 * 1024 * 1024

LOG2E = math.log2(math.e)
HEADS_PER_PAIR = LANES // MOBA_HEAD_DIM
MOBA_PAIRS_PER_STEP = 2
MOBA_HEADS_PER_STEP = HEADS_PER_PAIR * MOBA_PAIRS_PER_STEP
MOBA_GROUP = 2
MASK_LANES = 32
IN_PROJ_TILE = 512
VT_ROWS = MOBA_HEAD_DIM + 16

_MAX_EXACT = REL_BUCKETS // 2
_LAST_BUCKET_DIST = math.ceil(_MAX_EXACT * (REL_MAX_DIST / _MAX_EXACT) ** ((REL_BUCKETS - 1 - _MAX_EXACT) / (REL_BUCKETS - _MAX_EXACT)))
NEAR_BLOCKS = -(-(_LAST_BUCKET_DIST + MOBA_BLOCK - 1) // MOBA_BLOCK)

HGRN_CHUNK = 64
HGRN_LEVELS = (32, 16, 8, 4, 2, 1)


def _sigmoid(x):
    return 0.5 * jnp.tanh(0.5 * x) + 0.5


def _silu(x):
    return x * _sigmoid(x)


def _dot(a, b):
    return jnp.dot(a, b, preferred_element_type=F32)


def _dot_nt(a, b):
    return lax.dot_general(a, b, (((1,), (1,)), ((), ())), preferred_element_type=F32)


def _dot_tn(a, b):
    return lax.dot_general(a, b, (((0,), (0,)), ((), ())), preferred_element_type=F32)


def _bias_tiles_kernel(tab_ref, out_ref):
    h = pl.program_id(0)
    d = pl.program_id(1)
    span = 2 * MOBA_BLOCK
    lane = lax.broadcasted_iota(jnp.int32, (SUBLANES, span), 1)
    rel = d * MOBA_BLOCK + jnp.where(lane < MOBA_BLOCK, lane, lane - span)
    n = jnp.maximum(rel, 0)
    nf = jnp.maximum(n, _MAX_EXACT).astype(F32)
    large = _MAX_EXACT + (jnp.log(nf / _MAX_EXACT) / math.log(REL_MAX_DIST / _MAX_EXACT)
                          * (REL_BUCKETS - _MAX_EXACT)).astype(jnp.int32)
    large = jnp.minimum(large, REL_BUCKETS - 1)
    bucket = jnp.where(n < _MAX_EXACT, n, large)
    val = jnp.zeros((SUBLANES, span), F32)
    for b in range(REL_BUCKETS):
        val = jnp.where(bucket == b, tab_ref[b, h], val)
    val = jnp.where(rel >= 0, val * LOG2E, MASK_VALUE)
    val = jnp.where(d < NEAR_BLOCKS, val, 0.0)
    rows = jnp.broadcast_to(val[0:1], (MOBA_BLOCK, span))
    out_ref[0, 0] = pltpu.roll(rows, 0, 1, stride=1, stride_axis=0)[:, :MOBA_BLOCK]


def _bias_tiles(rel_bias):
    return pl.pallas_call(
        _bias_tiles_kernel,
        grid=(MOBA_HEADS, NEAR_BLOCKS + 1),
        in_specs=[pl.BlockSpec(memory_space=pltpu.SMEM)],
        out_specs=pl.BlockSpec((1, 1, MOBA_BLOCK, MOBA_BLOCK), lambda h, d: (h, d, 0, 0)),
        out_shape=jax.ShapeDtypeStruct((MOBA_HEADS, NEAR_BLOCKS + 1, MOBA_BLOCK, MOBA_BLOCK), F32),
        name="t5_bias_tiles",
    )(rel_bias.astype(F32))


def _split_bf16(x):
    hi = x.astype(BF16)
    lo = (x - hi.astype(F32)).astype(BF16)
    return hi, lo


def _moba_queries(blk, slot, pp, qt_pair, km_pair, tab_ref, qaug_ref):
    tq = MOBA_BLOCK
    scale = MOBA_HEAD_DIM ** -0.5 * LOG2E
    lane = lax.broadcasted_iota(jnp.int32, (MASK_LANES, LANES), 1)
    kblk = lax.broadcasted_iota(jnp.int32, (MASK_LANES, tq), 0)
    kblk_f = kblk.astype(F32)
    past = kblk < blk
    q_hi, q_lo = _split_bf16(qt_pair)
    for hh in range(HEADS_PER_PAIR):
        h = pp * HEADS_PER_PAIR + hh
        in_head = (lane >= hh * MOBA_HEAD_DIM) & (lane < (hh + 1) * MOBA_HEAD_DIM)
        k_hi, k_lo = _split_bf16(jnp.where(in_head, km_pair, 0.0))
        gate = _dot(k_hi, q_hi) + _dot(k_hi, q_lo) + _dot(k_lo, q_hi)
        gate = jnp.where(past, gate, -jnp.inf)
        sel = jnp.zeros((MASK_LANES, tq), jnp.bool_)
        for _ in range(MOBA_TOPK):
            best = jnp.max(gate, axis=0, keepdims=True)
            first = jnp.min(jnp.where(gate == best, kblk_f, float(MASK_LANES)), axis=0, keepdims=True)
            hit = kblk_f == first
            sel = sel | hit
            gate = jnp.where(hit, -jnp.inf, gate)
        sel = (sel & past) | (kblk == blk)
        far_bias = jnp.where(blk - kblk >= NEAR_BLOCKS, tab_ref[REL_BUCKETS - 1, h] * LOG2E, 0.0)
        mask_hi, mask_lo = _split_bf16(jnp.where(sel, far_bias, MASK_VALUE))
        q_rows = (qt_pair[hh * MOBA_HEAD_DIM:(hh + 1) * MOBA_HEAD_DIM] * scale).astype(BF16)
        parts = [q_rows, mask_hi, mask_lo] if hh == 0 else [mask_hi, mask_lo, q_rows]
        qaug_ref[0, h, slot] = jnp.concatenate(parts, axis=0)


def _in_proj_kernel(layer, tab_ref, x_ref, gain_ref, w_ref, qg_ref, kg_ref, lbl_ref, hmean_ref,
                    qaug_ref, k_ref, vt_ref, ag_ref, hq_ref, hk_ref, hv_ref, hg_ref,
                    bg_ref, ga_ref, gb_ref, kmean_ref):
    tile_blocks = x_ref.shape[1] // MOBA_BLOCK
    first_blk = pl.program_id(1) * tile_blocks
    x = x_ref[0]
    xn = x * lax.rsqrt(jnp.mean(x * x, axis=-1, keepdims=True) + RMS_EPS) * gain_ref[...]
    xn = xn.astype(BF16)

    def seg(start, width):
        return _dot(xn, w_ref[0, :, start:start + width])

    def head_rms(t, gain):
        ms = _dot((t * t).astype(BF16), hmean_ref[...])
        return t * lax.rsqrt(ms + RMS_EPS) * gain

    w = MOBA_WIDTH
    qn = head_rms(seg(0, w), qg_ref[...])
    kn = head_rms(seg(w, w), kg_ref[...])

    @pl.when(first_blk == 0)
    def _():
        kmean_ref[...] = jnp.zeros_like(kmean_ref)

    for j in range(tile_blocks):
        rows = slice(j * MOBA_BLOCK, (j + 1) * MOBA_BLOCK)
        qt = qn[rows].T
        kmean = kmean_ref[...]
        for pp in range(MOBA_HEADS // HEADS_PER_PAIR):
            lanes = slice(pp * LANES, (pp + 1) * LANES)
            _moba_queries(first_blk + j, j, pp, qt[lanes], kmean[:, lanes], tab_ref, qaug_ref)
        kmean_ref[pl.ds(first_blk + j, 1), :] = jnp.mean(kn[rows], axis=0, keepdims=True)

    lane = lax.broadcasted_iota(jnp.int32, (x.shape[0], LANES), 1)
    row = lax.broadcasted_iota(jnp.int32, (x.shape[0], LANES), 0)
    row_blk = first_blk + jnp.right_shift(row, MOBA_BLOCK.bit_length() - 1)
    block_hot = jnp.where((lane & (MASK_LANES - 1)) == row_blk, 1.0, 0.0)
    for pp in range(MOBA_HEADS // HEADS_PER_PAIR):
        kp = kn[:, pp * LANES:(pp + 1) * LANES]
        for hh, own_lanes in enumerate((lane < MOBA_HEAD_DIM, lane >= MOBA_HEAD_DIM)):
            k_head = jnp.where(own_lanes, kp, block_hot).astype(BF16)
            k_ref[0, HEADS_PER_PAIR * pp + hh] = pltpu.bitcast(k_head, jnp.uint32)
    vt = seg(2 * w, w).T
    extra = lax.broadcasted_iota(jnp.int32, (VT_ROWS - MOBA_HEAD_DIM, x.shape[0]), 0)
    ones_row = jnp.where(extra == 0, 1.0, 0.0).astype(BF16)
    for h in range(MOBA_HEADS):
        v_head = vt[h * MOBA_HEAD_DIM:(h + 1) * MOBA_HEAD_DIM].astype(BF16)
        vt_ref[0, h, :MOBA_HEAD_DIM // 2] = pltpu.bitcast(v_head, jnp.uint32)
        vt_ref[0, h, MOBA_HEAD_DIM // 2:] = pltpu.bitcast(ones_row, jnp.uint32)
    ag_ref[0] = _silu(seg(3 * w, w)).astype(BF16)

    base = 4 * w
    hw = HGRN_WIDTH
    hq_ref[0] = _silu(seg(base, hw))
    lbl = lbl_ref[...]
    e = jnp.exp(lbl - jnp.max(lbl, axis=0, keepdims=True))
    sm = e / jnp.sum(e, axis=0, keepdims=True)
    lb = jnp.zeros((1, hw), F32)
    for j in range(1, layer + 1):
        lb = lb + sm[j:j + 1, :]
    z = seg(base + hw, hw)
    e = jnp.exp(-jnp.abs(z))
    pos = z >= 0.0
    inv = 1.0 / (1.0 + e)
    clipped = jnp.where(pos, e, jnp.minimum(1.0 / e, math.exp(EXP_CLIP)))
    hg_ref[0] = jnp.minimum(z, 0.0) * LOG2E + jnp.log2((1.0 + lb * clipped) * inv)
    hk_ref[0] = (1.0 - lb) * jnp.where(pos, e, 1.0) * inv
    hv_ref[0] = seg(base + 2 * hw, hw)
    bg_ref[0] = _silu(seg(base + 3 * hw, hw)).astype(BF16)

    base = base + 4 * hw
    d = x.shape[-1]
    ga_ref[0] = _sigmoid(seg(base, d)).astype(BF16)
    gb_ref[0] = _sigmoid(seg(base + d, d)).astype(BF16)


def _in_proj(layer, rel_bias, h, norm_gain, w_in, q_gain, k_gain, lb_logits, hmean):
    b, s, d = h.shape
    depth = w_in.shape[0]
    tm = IN_PROJ_TILE
    nt = s // tm
    tile_blocks = tm // MOBA_BLOCK
    nb = s // MOBA_BLOCK
    assert s % tm == 0 and nb <= MASK_LANES and HEADS_PER_PAIR == 2
    in_width = w_in.shape[-1]
    tok = lambda width: pl.BlockSpec((1, tm, width), lambda bi, ti: (bi, ti, 0))
    const2 = lambda shape: pl.BlockSpec(shape, lambda bi, ti: (0, 0))
    tok_shape = lambda width, dtype: jax.ShapeDtypeStruct((b, s, width), dtype)
    outs = (
        (pl.BlockSpec((1, MOBA_HEADS, tile_blocks, LANES, MOBA_BLOCK), lambda bi, ti: (bi, 0, ti, 0, 0)),
         jax.ShapeDtypeStruct((b, MOBA_HEADS, nb, LANES, MOBA_BLOCK), BF16)),
        (pl.BlockSpec((1, MOBA_HEADS, tm // 2, LANES), lambda bi, ti: (bi, 0, ti, 0)),
         jax.ShapeDtypeStruct((b, MOBA_HEADS, s // 2, LANES), jnp.uint32)),
        (pl.BlockSpec((1, MOBA_HEADS, VT_ROWS // 2, tm), lambda bi, ti: (bi, 0, 0, ti)),
         jax.ShapeDtypeStruct((b, MOBA_HEADS, VT_ROWS // 2, s), jnp.uint32)),
        (tok(MOBA_WIDTH), tok_shape(MOBA_WIDTH, BF16)),
        (tok(HGRN_WIDTH), tok_shape(HGRN_WIDTH, F32)),
        (tok(HGRN_WIDTH), tok_shape(HGRN_WIDTH, F32)),
        (tok(HGRN_WIDTH), tok_shape(HGRN_WIDTH, F32)),
        (tok(HGRN_WIDTH), tok_shape(HGRN_WIDTH, F32)),
        (tok(HGRN_WIDTH), tok_shape(HGRN_WIDTH, BF16)),
        (tok(d), tok_shape(d, BF16)),
        (tok(d), tok_shape(d, BF16)),
    )
    return pl.pallas_call(
        functools.partial(_in_proj_kernel, layer),
        grid=(b, nt),
        in_specs=[
            pl.BlockSpec(memory_space=pltpu.SMEM),
            tok(d),
            const2((1, d)),
            pl.BlockSpec((1, d, in_width), lambda bi, ti: (layer, 0, 0), pipeline_mode=pl.Buffered(1)),
            const2((1, MOBA_WIDTH)),
            const2((1, MOBA_WIDTH)),
            const2((depth, HGRN_WIDTH)),
            const2((MOBA_WIDTH, MOBA_WIDTH)),
        ],
        out_specs=[o[0] for o in outs],
        out_shape=[o[1] for o in outs],
        scratch_shapes=[pltpu.VMEM((MASK_LANES, MOBA_WIDTH), F32)],
        compiler_params=pltpu.CompilerParams(
            dimension_semantics=("arbitrary", "arbitrary"), vmem_limit_bytes=VMEM_LIMIT_BYTES),
        name=f"in_proj_l{layer}",
    )(rel_bias.astype(F32), h, norm_gain, w_in, q_gain, k_gain, lb_logits, hmean)


def _moba_kernel(qaug_ref, qnext_ref, k_ref, vt_ref, bias_ref, out_ref, s0_ref, s1_ref, max0_ref, max1_ref):
    blk = pl.program_id(2)
    tq = MOBA_BLOCK
    group_rows = MOBA_GROUP * MOBA_BLOCK

    def score_head(g, h, bufs, near, q_ref=qaug_ref, q_blk=blk):
        s_ref, max_ref = bufs
        words = pl.ds(pl.multiple_of(g * (group_rows // 2), group_rows // 2), group_rows // 2)
        k_rows = pltpu.bitcast(k_ref[0, h, words, :], BF16)
        s = _dot(k_rows, q_ref[0, h, 0])
        if near:
            parts = []
            for u in range(MOBA_GROUP):
                dist = q_blk - (g * MOBA_GROUP + u)
                tile = bias_ref[h, jnp.where((dist >= 0) & (dist < NEAR_BLOCKS), dist, NEAR_BLOCKS)]
                parts.append(s[u * MOBA_BLOCK:(u + 1) * MOBA_BLOCK] + tile)
            s = jnp.concatenate(parts, axis=0)
        s_ref[h] = s
        max_ref[h] = jnp.max(s.reshape(group_rows // SUBLANES, SUBLANES, tq), axis=0)

    def softmax_head(g, h, bufs, m, acc):
        s_ref, max_ref = bufs
        rows = pl.ds(pl.multiple_of(g * group_rows, group_rows), group_rows)
        m_new = jnp.maximum(m, jnp.max(max_ref[h], axis=0, keepdims=True))
        p = jnp.exp2(s_ref[h] - m_new).astype(BF16)
        acc = jnp.exp2(m - m_new) * acc + _dot(pltpu.bitcast(vt_ref[0, h, :, rows], BF16), p)
        return m_new, acc

    heads = range(MOBA_HEADS_PER_STEP)
    state = []
    for _ in heads:
        state += [jnp.full((1, tq), MASK_VALUE, F32), jnp.zeros((VT_ROWS, tq), F32)]
    bufs = ((s0_ref, max0_ref), (s1_ref, max1_ref))

    def group_pair(i, state, near_next):
        state = list(state)
        for parity in range(2):
            for h in heads:
                state[2 * h], state[2 * h + 1] = softmax_head(
                    2 * i + parity, h, bufs[parity], state[2 * h], state[2 * h + 1])
                score_head(2 * i + 2 + parity, h, bufs[parity], near_next)
        return state

    pair_blocks = 2 * MOBA_GROUP
    last = blk // pair_blocks
    n_far = jnp.maximum(jnp.maximum(blk - (NEAR_BLOCKS - 1), 0) // pair_blocks - 1, 0)

    @pl.when(blk == 0)
    def _():
        for parity in range(2):
            for h in heads:
                score_head(parity, h, bufs[parity], True)

    def pairs(lo, hi, state, near_next):
        def two(j, st):
            return tuple(group_pair(lo + 2 * j + 1, group_pair(lo + 2 * j, st, near_next), near_next))

        count = hi - lo
        state = lax.fori_loop(0, count // 2, two, tuple(state))
        return lax.cond(count % 2 == 1, lambda st: tuple(group_pair(hi - 1, st, near_next)),
                        lambda st: tuple(st), state)

    state = pairs(0, n_far, state, False)
    state = pairs(n_far, last, state, True)

    def last_pair(groups):
        def run(st):
            st = list(st)
            for parity in range(groups):
                for h in heads:
                    st[2 * h], st[2 * h + 1] = softmax_head(
                        2 * last + parity, h, bufs[parity], st[2 * h], st[2 * h + 1])
                    for nxt in range(parity, 2 if parity == groups - 1 else parity + 1):
                        score_head(nxt, h, bufs[nxt], True, qnext_ref, blk + 1)
            return tuple(st)
        return run

    second = (blk % pair_blocks) >= MOBA_GROUP
    state = lax.cond(second, last_pair(2), last_pair(1), tuple(state))

    accs = state[1::2]
    out_t = jnp.concatenate([a[:MOBA_HEAD_DIM] / a[MOBA_HEAD_DIM:MOBA_HEAD_DIM + 1] for a in accs], axis=0)
    out_ref[0] = out_t.T.astype(out_ref.dtype)


def _moba(qaug, k, vt, bias_tiles):
    b, _, _, s = vt.shape
    nb = s // MOBA_BLOCK
    assert nb % (2 * MOBA_GROUP) == 0
    hs = MOBA_HEADS_PER_STEP
    width = hs * MOBA_HEAD_DIM
    return pl.pallas_call(
        _moba_kernel,
        grid=(b, MOBA_HEADS // hs, nb),
        in_specs=[
            pl.BlockSpec((1, hs, 1, LANES, MOBA_BLOCK), lambda bi, pi, ti: (bi, pi, ti, 0, 0)),
            pl.BlockSpec((1, hs, 1, LANES, MOBA_BLOCK),
                         lambda bi, pi, ti: (bi, pi, jnp.minimum(ti + 1, nb - 1), 0, 0)),
            pl.BlockSpec((1, hs, s // 2, LANES), lambda bi, pi, ti: (bi, pi, 0, 0)),
            pl.BlockSpec((1, hs, VT_ROWS // 2, s), lambda bi, pi, ti: (bi, pi, 0, 0)),
            pl.BlockSpec((hs, NEAR_BLOCKS + 1, MOBA_BLOCK, MOBA_BLOCK),
                         lambda bi, pi, ti: (pi, 0, 0, 0), pipeline_mode=pl.Buffered(1)),
        ],
        out_specs=pl.BlockSpec((1, MOBA_BLOCK, width), lambda bi, pi, ti: (bi, ti, pi)),
        out_shape=jax.ShapeDtypeStruct((b, s, MOBA_WIDTH), BF16),
        scratch_shapes=[pltpu.VMEM((hs, MOBA_GROUP * MOBA_BLOCK, MOBA_BLOCK), F32),
                        pltpu.VMEM((hs, MOBA_GROUP * MOBA_BLOCK, MOBA_BLOCK), F32),
                        pltpu.VMEM((hs, SUBLANES, MOBA_BLOCK), F32),
                        pltpu.VMEM((hs, SUBLANES, MOBA_BLOCK), F32)],
        compiler_params=pltpu.CompilerParams(
            dimension_semantics=("arbitrary", "arbitrary", "arbitrary"), vmem_limit_bytes=VMEM_LIMIT_BYTES),
        name="moba_attention",
    )(qaug, qaug, k, vt, bias_tiles)


def _chunk_cumsum(g):
    groups = HGRN_CHUNK // SUBLANES
    g3 = g.reshape(groups, SUBLANES, g.shape[-1])
    sub = lax.broadcasted_iota(jnp.int32, g3.shape, 1)
    shift = 1
    while shift < SUBLANES:
        g3 = g3 + jnp.where(sub >= shift, pltpu.roll(g3, shift, axis=1), 0.0)
        shift *= 2
    rows = []
    run = None
    for i in range(groups):
        cur = g3[i] if run is None else g3[i] + run
        rows.append(cur)
        run = cur[SUBLANES - 1:SUBLANES, :]
    return jnp.concatenate(rows, axis=0)


def _hgrn_masks():
    c = HGRN_CHUNK
    row = lax.broadcasted_iota(jnp.int32, (c, 1), 0)
    rt = lax.broadcasted_iota(jnp.int32, (c, c), 0)
    cs = lax.broadcasted_iota(jnp.int32, (c, c), 1)
    levels = []
    for half in HGRN_LEVELS:
        shift = int(math.log2(2 * half))
        valid = ((rt >> shift) == (cs >> shift)) & ((rt & half) != 0) & ((cs & half) == 0)
        levels.append(((row & half) != 0, valid))
    return levels, rt == cs


def _hgrn_chunk_matmuls(q, k, v, g2, state_t, level_masks):
    c = HGRN_CHUNK
    cum = _chunk_cumsum(g2)
    last = cum[c - 1:c, :]
    v16 = v.astype(BF16)

    o = _dot_nt((q * jnp.exp2(cum)).astype(BF16), state_t.astype(BF16))
    k_end = (k * jnp.exp2(last - cum)).astype(BF16)
    new_state = state_t * jnp.exp2(last) + _dot_tn(v16, k_end)

    scores = jnp.zeros((c, c), F32)
    groups = c // SUBLANES
    width = cum.shape[-1]
    cum3 = cum.reshape(groups, SUBLANES, width)
    sub = lax.broadcasted_iota(jnp.int32, (groups, SUBLANES, width), 1)

    def row_of_group(r):
        return jnp.broadcast_to(cum3[:, r:r + 1, :], cum3.shape)

    for half, (is_q, valid) in zip(HGRN_LEVELS, level_masks):
        if half == 1:
            exponent = jnp.where(is_q, g2, 0.0)
        else:
            if half >= SUBLANES:
                ref = jnp.concatenate(
                    [jnp.broadcast_to(cum[lo + half - 1:lo + half, :], (2 * half, width))
                     for lo in range(0, c, 2 * half)], axis=0)
            elif half == 4:
                ref = row_of_group(3).reshape(c, width)
            else:
                ref = jnp.where(sub < 4, row_of_group(1), row_of_group(5)).reshape(c, width)
            exponent = jnp.where(is_q, cum - ref, ref - cum)
        x = (jnp.where(is_q, q, k) * jnp.exp2(exponent)).astype(BF16)
        scores = scores + jnp.where(valid, _dot_nt(x, x), 0.0)
    return o, new_state, scores, v16


def _hgrn_kernel(q_ref, k_ref, v_ref, g_ref, out_ref, state_ref):
    @pl.when(pl.program_id(1) == 0)
    def _():
        state_ref[...] = jnp.zeros_like(state_ref)

    n_chunks = q_ref.shape[1] // HGRN_CHUNK
    masks = _hgrn_masks()

    level_masks, on_diagonal = masks

    def chunk(ci, carry):
        rows = pl.ds(pl.multiple_of(ci * HGRN_CHUNK, HGRN_CHUNK), HGRN_CHUNK)
        partial = []
        for h in range(HGRN_HEADS):
            cols = slice(h * HGRN_DIM, (h + 1) * HGRN_DIM)
            q, k = q_ref[0, rows, cols], k_ref[0, rows, cols]
            o, new_state, scores, v16 = _hgrn_chunk_matmuls(
                q, k, v_ref[0, rows, cols], g_ref[0, rows, cols], state_ref[h], level_masks)
            state_ref[h] = new_state
            partial.append((jnp.sum(q * k, axis=-1, keepdims=True), o, scores, v16))
        for h, (self_score, o, scores, v16) in enumerate(partial):
            scores = jnp.where(on_diagonal, self_score, scores)
            out = o + _dot(scores.astype(BF16), v16)
            out_ref[0, rows, h * HGRN_DIM:(h + 1) * HGRN_DIM] = out.astype(out_ref.dtype)
        return carry

    lax.fori_loop(0, n_chunks, chunk, 0, unroll=8)


def _hgrn(hq, hk, hv, hg):
    b, s, w = hq.shape
    ts = min(s, 512)
    spec = pl.BlockSpec((1, ts, w), lambda bi, ti: (bi, ti, 0))
    return pl.pallas_call(
        _hgrn_kernel,
        grid=(b, s // ts),
        in_specs=[spec, spec, spec, spec],
        out_specs=spec,
        out_shape=jax.ShapeDtypeStruct((b, s, w), BF16),
        scratch_shapes=[pltpu.VMEM((HGRN_HEADS, HGRN_DIM, HGRN_DIM), F32)],
        compiler_params=pltpu.CompilerParams(
            dimension_semantics=("arbitrary", "arbitrary"), vmem_limit_bytes=VMEM_LIMIT_BYTES),
        name="hgrn2_recurrence",
    )(hq, hk, hv, hg)


def _merge_kernel(h_ref, ya_ref, ag_ref, yb_ref, bg_ref, ga_ref, gb_ref, p_ref, og_ref,
                  wa_ref, wb_ref, wo_ref, wp_ref, wg_ref, out_ref):
    ya = _dot(ya_ref[0] * ag_ref[0], wa_ref[0])

    yb = yb_ref[0].astype(F32)
    normed = []
    for h in range(HGRN_HEADS):
        t = yb[:, h * HGRN_DIM:(h + 1) * HGRN_DIM]
        normed.append(t * lax.rsqrt(jnp.mean(t * t, axis=-1, keepdims=True) + RMS_EPS))
    yb = jnp.concatenate(normed, axis=-1) * og_ref[...]
    yb = _dot((yb * bg_ref[0]).astype(BF16), wb_ref[0])

    merged = ga_ref[0] * ya + gb_ref[0] * yb
    h = h_ref[0] + _dot(merged.astype(BF16), wo_ref[0])
    ple = _dot(p_ref[0, 0].astype(BF16), wp_ref[0]) * _sigmoid(_dot(h.astype(BF16), wg_ref[0]))
    out_ref[0] = h + ple


def _merge(layer, h, ya, ag, yb, bg, ga, gb, p, out_gain, w_up_a, w_up_b, w_out, w_ple, w_ple_gate):
    b, s, d = h.shape
    tm = min(s, 512)
    tok = lambda width: pl.BlockSpec((1, tm, width), lambda bi, ti: (bi, ti, 0))
    wspec = lambda w: pl.BlockSpec((1,) + w.shape[1:], lambda bi, ti: (layer, 0, 0), pipeline_mode=pl.Buffered(1))
    return pl.pallas_call(
        _merge_kernel,
        grid=(b, s // tm),
        in_specs=[
            tok(d), tok(MOBA_WIDTH), tok(MOBA_WIDTH), tok(HGRN_WIDTH), tok(HGRN_WIDTH), tok(d), tok(d),
            pl.BlockSpec((1, 1, tm, p.shape[-1]), lambda bi, ti: (layer, bi, ti, 0)),
            pl.BlockSpec((1, HGRN_WIDTH), lambda bi, ti: (0, 0)),
            wspec(w_up_a), wspec(w_up_b), wspec(w_out), wspec(w_ple), wspec(w_ple_gate),
        ],
        out_specs=tok(d),
        out_shape=jax.ShapeDtypeStruct((b, s, d), F32),
        compiler_params=pltpu.CompilerParams(
            dimension_semantics=("arbitrary", "arbitrary"), vmem_limit_bytes=VMEM_LIMIT_BYTES),
        name=f"merge_l{layer}",
    )(h, ya, ag, yb, bg, ga, gb, p, out_gain, w_up_a, w_up_b, w_out, w_ple, w_ple_gate)


def kernel(x, p, norm_gain, w_in, q_norm_gain, k_norm_gain, rel_bias, hgrn_lb_logits, hgrn_out_gain,
           w_up_a, w_up_b, w_out, w_ple, w_ple_gate):
    b, s, d = x.shape
    depth = w_in.shape[0]
    assert s % MOBA_BLOCK == 0 and s % HGRN_CHUNK == 0

    w_in, w_up_a, w_up_b, w_out, w_ple, w_ple_gate = (
        w.astype(BF16) for w in (w_in, w_up_a, w_up_b, w_out, w_ple, w_ple_gate))
    head_id = jnp.arange(MOBA_WIDTH) // MOBA_HEAD_DIM
    hmean = ((head_id[:, None] == head_id[None, :]).astype(F32) / MOBA_HEAD_DIM).astype(BF16)
    tile = lambda g: jnp.tile(g.astype(F32), MOBA_HEADS)[None, :]

    bias_tiles = _bias_tiles(rel_bias)
    h = x
    for i in range(depth):
        qaug, k, vt, ag, hq, hk, hv, hg, bg, ga, gb = _in_proj(
            i, rel_bias, h, norm_gain[i][None, :].astype(F32), w_in, tile(q_norm_gain[i]), tile(k_norm_gain[i]),
            hgrn_lb_logits.astype(F32), hmean)
        ya = _moba(qaug, k, vt, bias_tiles)
        yb = _hgrn(hq, hk, hv, hg)
        h = _merge(i, h, ya, ag, yb, bg, ga, gb, p, hgrn_out_gain[i][None, :].astype(F32),
                   w_up_a, w_up_b, w_out, w_ple, w_ple_gate)
    return h
```

```python
import functools
import math

import jax
import jax.numpy as jnp
from jax import lax
from jax.experimental import pallas as pl
from jax.experimental.pallas import tpu as pltpu

F32 = jnp.float32
BF16 = jnp.bfloat16

MOBA_HEADS = 8
MOBA_HEAD_DIM = 64
MOBA_WIDTH = MOBA_HEADS * MOBA_HEAD_DIM
MOBA_BLOCK = 256
MOBA_TOPK = 3
HGRN_HEADS = 4
HGRN_DIM = 128
HGRN_WIDTH = HGRN_HEADS * HGRN_DIM
EXP_CLIP = 30.0
REL_BUCKETS = 32
REL_MAX_DIST = 2048
RMS_EPS = 1e-6
MASK_VALUE = -1e30

LANES = 128
SUBLANES = 8
VMEM_LIMIT_BYTES = 56 * 1024 * 1024

LOG2E = math.log2(math.e)
HEADS_PER_PAIR = LANES // MOBA_HEAD_DIM
MOBA_PAIRS_PER_STEP = 2
MOBA_HEADS_PER_STEP = HEADS_PER_PAIR * MOBA_PAIRS_PER_STEP
MOBA_GROUP = 2
MASK_LANES = 32
IN_PROJ_TILE = 512
VT_ROWS = MOBA_HEAD_DIM + 16

_MAX_EXACT = REL_BUCKETS // 2
_LAST_BUCKET_DIST = math.ceil(_MAX_EXACT * (REL_MAX_DIST / _MAX_EXACT) ** ((REL_BUCKETS - 1 - _MAX_EXACT) / (REL_BUCKETS - _MAX_EXACT)))
NEAR_BLOCKS = -(-(_LAST_BUCKET_DIST + MOBA_BLOCK - 1) // MOBA_BLOCK)

HGRN_CHUNK = 128
HGRN_LEVELS = (64, 32, 16, 8, 4, 2, 1)


def _sigmoid(x):
    return 0.5 * jnp.tanh(0.5 * x) + 0.5


def _silu(x):
    return x * _sigmoid(x)


def _dot(a, b):
    return jnp.dot(a, b, preferred_element_type=F32)


def _dot_nt(a, b):
    return lax.dot_general(a, b, (((1,), (1,)), ((), ())), preferred_element_type=F32)


def _dot_tn(a, b):
    return lax.dot_general(a, b, (((0,), (0,)), ((), ())), preferred_element_type=F32)


def _bias_tiles_kernel(tab_ref, out_ref):
    h = pl.program_id(0)
    span = 2 * MOBA_BLOCK
    lane = lax.broadcasted_iota(jnp.int32, (SUBLANES, span), 1)
    for d in range(NEAR_BLOCKS):
        rel = d * MOBA_BLOCK + jnp.where(lane < MOBA_BLOCK, lane, lane - span)
        n = jnp.maximum(rel, 0)
        nf = jnp.maximum(n, _MAX_EXACT).astype(F32)
        large = _MAX_EXACT + (jnp.log(nf / _MAX_EXACT) / math.log(REL_MAX_DIST / _MAX_EXACT)
                              * (REL_BUCKETS - _MAX_EXACT)).astype(jnp.int32)
        large = jnp.minimum(large, REL_BUCKETS - 1)
        bucket = jnp.where(n < _MAX_EXACT, n, large)
        val = jnp.zeros((SUBLANES, span), F32)
        for b in range(REL_BUCKETS):
            val = jnp.where(bucket == b, tab_ref[b, h], val)
        val = jnp.where(rel >= 0, val * LOG2E, MASK_VALUE)
        rows = jnp.broadcast_to(val[0:1], (MOBA_BLOCK, span))
        out_ref[0, d] = pltpu.roll(rows, 0, 1, stride=1, stride_axis=0)[:, :MOBA_BLOCK]
    out_ref[0, NEAR_BLOCKS] = jnp.zeros((MOBA_BLOCK, MOBA_BLOCK), F32)


def _bias_tiles(rel_bias):
    return pl.pallas_call(
        _bias_tiles_kernel,
        grid=(MOBA_HEADS,),
        in_specs=[pl.BlockSpec(memory_space=pltpu.SMEM)],
        out_specs=pl.BlockSpec((1, NEAR_BLOCKS + 1, MOBA_BLOCK, MOBA_BLOCK), lambda h: (h, 0, 0, 0)),
        out_shape=jax.ShapeDtypeStruct((MOBA_HEADS, NEAR_BLOCKS + 1, MOBA_BLOCK, MOBA_BLOCK), F32),
        name="t5_bias_tiles",
    )(rel_bias.astype(F32))


def _split_bf16(x):
    hi = x.astype(BF16)
    lo = (x - hi.astype(F32)).astype(BF16)
    return hi, lo


def _moba_queries(blk, slot, pp, qt_pair, km_pair, tab_ref, qaug_ref):
    tq = MOBA_BLOCK
    scale = MOBA_HEAD_DIM ** -0.5 * LOG2E
    lane = lax.broadcasted_iota(jnp.int32, (MASK_LANES, LANES), 1)
    kblk = lax.broadcasted_iota(jnp.int32, (MASK_LANES, tq), 0)
    kblk_f = kblk.astype(F32)
    past = kblk < blk
    q_hi, q_lo = _split_bf16(qt_pair)
    for hh in range(HEADS_PER_PAIR):
        h = pp * HEADS_PER_PAIR + hh
        in_head = (lane >= hh * MOBA_HEAD_DIM) & (lane < (hh + 1) * MOBA_HEAD_DIM)
        k_hi, k_lo = _split_bf16(jnp.where(in_head, km_pair, 0.0))
        gate = _dot(k_hi, q_hi) + _dot(k_hi, q_lo) + _dot(k_lo, q_hi)
        gate = jnp.where(past, gate, -jnp.inf)
        sel = jnp.zeros((MASK_LANES, tq), jnp.bool_)
        for _ in range(MOBA_TOPK):
            best = jnp.max(gate, axis=0, keepdims=True)
            first = jnp.min(jnp.where(gate == best, kblk_f, float(MASK_LANES)), axis=0, keepdims=True)
            hit = kblk_f == first
            sel = sel | hit
            gate = jnp.where(hit, -jnp.inf, gate)
        sel = (sel & past) | (kblk == blk)
        far_bias = jnp.where(blk - kblk >= NEAR_BLOCKS, tab_ref[REL_BUCKETS - 1, h] * LOG2E, 0.0)
        mask_hi, mask_lo = _split_bf16(jnp.where(sel, far_bias, MASK_VALUE))
        q_rows = (qt_pair[hh * MOBA_HEAD_DIM:(hh + 1) * MOBA_HEAD_DIM] * scale).astype(BF16)
        parts = [q_rows, mask_hi, mask_lo] if hh == 0 else [mask_hi, mask_lo, q_rows]
        qaug_ref[0, h, slot] = jnp.concatenate(parts, axis=0)


def _in_proj_kernel(layer, tab_ref, x_ref, gain_ref, w_ref, qg_ref, kg_ref, lbl_ref, hmean_ref,
                    qaug_ref, k_ref, vt_ref, ag_ref, hq_ref, hk_ref, hv_ref, hg_ref,
                    bg_ref, ga_ref, gb_ref, kmean_ref):
    tile_blocks = x_ref.shape[1] // MOBA_BLOCK
    first_blk = pl.program_id(1) * tile_blocks
    x = x_ref[0]
    xn = x * lax.rsqrt(jnp.mean(x * x, axis=-1, keepdims=True) + RMS_EPS) * gain_ref[...]
    xn = xn.astype(BF16)

    def seg(start, width):
        return _dot(xn, w_ref[0, :, start:start + width])

    def head_rms(t, gain):
        ms = _dot((t * t).astype(BF16), hmean_ref[...])
        return t * lax.rsqrt(ms + RMS_EPS) * gain

    w = MOBA_WIDTH
    qn = head_rms(seg(0, w), qg_ref[...])
    kn = head_rms(seg(w, w), kg_ref[...])

    @pl.when(first_blk == 0)
    def _():
        kmean_ref[...] = jnp.zeros_like(kmean_ref)

    for j in range(tile_blocks):
        rows = slice(j * MOBA_BLOCK, (j + 1) * MOBA_BLOCK)
        qt = qn[rows].T
        kmean = kmean_ref[...]
        for pp in range(MOBA_HEADS // HEADS_PER_PAIR):
            lanes = slice(pp * LANES, (pp + 1) * LANES)
            _moba_queries(first_blk + j, j, pp, qt[lanes], kmean[:, lanes], tab_ref, qaug_ref)
        kmean_ref[pl.ds(first_blk + j, 1), :] = jnp.mean(kn[rows], axis=0, keepdims=True)

    lane = lax.broadcasted_iota(jnp.int32, (x.shape[0], LANES), 1)
    row = lax.broadcasted_iota(jnp.int32, (x.shape[0], LANES), 0)
    row_blk = first_blk + jnp.right_shift(row, MOBA_BLOCK.bit_length() - 1)
    block_hot = jnp.where((lane & (MASK_LANES - 1)) == row_blk, 1.0, 0.0)
    for pp in range(MOBA_HEADS // HEADS_PER_PAIR):
        kp = kn[:, pp * LANES:(pp + 1) * LANES]
        for hh, own_lanes in enumerate((lane < MOBA_HEAD_DIM, lane >= MOBA_HEAD_DIM)):
            k_head = jnp.where(own_lanes, kp, block_hot).astype(BF16)
            k_ref[0, HEADS_PER_PAIR * pp + hh] = pltpu.bitcast(k_head, jnp.uint32)
    vt = seg(2 * w, w).T
    extra = lax.broadcasted_iota(jnp.int32, (VT_ROWS - MOBA_HEAD_DIM, x.shape[0]), 0)
    ones_row = jnp.where(extra == 0, 1.0, 0.0).astype(BF16)
    for h in range(MOBA_HEADS):
        v_head = vt[h * MOBA_HEAD_DIM:(h + 1) * MOBA_HEAD_DIM].astype(BF16)
        vt_ref[0, h, :MOBA_HEAD_DIM // 2] = pltpu.bitcast(v_head, jnp.uint32)
        vt_ref[0, h, MOBA_HEAD_DIM // 2:] = pltpu.bitcast(ones_row, jnp.uint32)
    ag_ref[0] = _silu(seg(3 * w, w)).astype(BF16)

    base = 4 * w
    hw = HGRN_WIDTH
    hq_ref[0] = _silu(seg(base, hw))
    lbl = lbl_ref[...]
    e = jnp.exp(lbl - jnp.max(lbl, axis=0, keepdims=True))
    sm = e / jnp.sum(e, axis=0, keepdims=True)
    lb = jnp.zeros((1, hw), F32)
    for j in range(1, layer + 1):
        lb = lb + sm[j:j + 1, :]
    z = seg(base + hw, hw)
    e = jnp.exp(-jnp.abs(z))
    pos = z >= 0.0
    inv = 1.0 / (1.0 + e)
    clipped = jnp.where(pos, e, jnp.minimum(1.0 / e, math.exp(EXP_CLIP)))
    hg_ref[0] = jnp.minimum(z, 0.0) * LOG2E + jnp.log2((1.0 + lb * clipped) * inv)
    hk_ref[0] = (1.0 - lb) * jnp.where(pos, e, 1.0) * inv
    hv_ref[0] = seg(base + 2 * hw, hw)
    bg_ref[0] = _silu(seg(base + 3 * hw, hw)).astype(BF16)

    base = base + 4 * hw
    d = x.shape[-1]
    ga_ref[0] = _sigmoid(seg(base, d)).astype(BF16)
    gb_ref[0] = _sigmoid(seg(base + d, d)).astype(BF16)


def _in_proj(layer, rel_bias, h, norm_gain, w_in, q_gain, k_gain, lb_logits, hmean):
    b, s, d = h.shape
    depth = w_in.shape[0]
    tm = IN_PROJ_TILE
    nt = s // tm
    tile_blocks = tm // MOBA_BLOCK
    nb = s // MOBA_BLOCK
    assert s % tm == 0 and nb <= MASK_LANES and HEADS_PER_PAIR == 2
    in_width = w_in.shape[-1]
    tok = lambda width: pl.BlockSpec((1, tm, width), lambda bi, ti: (bi, ti, 0))
    const2 = lambda shape: pl.BlockSpec(shape, lambda bi, ti: (0, 0))
    tok_shape = lambda width, dtype: jax.ShapeDtypeStruct((b, s, width), dtype)
    outs = (
        (pl.BlockSpec((1, MOBA_HEADS, tile_blocks, LANES, MOBA_BLOCK), lambda bi, ti: (bi, 0, ti, 0, 0)),
         jax.ShapeDtypeStruct((b, MOBA_HEADS, nb, LANES, MOBA_BLOCK), BF16)),
        (pl.BlockSpec((1, MOBA_HEADS, tm // 2, LANES), lambda bi, ti: (bi, 0, ti, 0)),
         jax.ShapeDtypeStruct((b, MOBA_HEADS, s // 2, LANES), jnp.uint32)),
        (pl.BlockSpec((1, MOBA_HEADS, VT_ROWS // 2, tm), lambda bi, ti: (bi, 0, 0, ti)),
         jax.ShapeDtypeStruct((b, MOBA_HEADS, VT_ROWS // 2, s), jnp.uint32)),
        (tok(MOBA_WIDTH), tok_shape(MOBA_WIDTH, BF16)),
        (tok(HGRN_WIDTH), tok_shape(HGRN_WIDTH, F32)),
        (tok(HGRN_WIDTH), tok_shape(HGRN_WIDTH, F32)),
        (tok(HGRN_WIDTH), tok_shape(HGRN_WIDTH, F32)),
        (tok(HGRN_WIDTH), tok_shape(HGRN_WIDTH, F32)),
        (tok(HGRN_WIDTH), tok_shape(HGRN_WIDTH, BF16)),
        (tok(d), tok_shape(d, BF16)),
        (tok(d), tok_shape(d, BF16)),
    )
    return pl.pallas_call(
        functools.partial(_in_proj_kernel, layer),
        grid=(b, nt),
        in_specs=[
            pl.BlockSpec(memory_space=pltpu.SMEM),
            tok(d),
            const2((1, d)),
            pl.BlockSpec((1, d, in_width), lambda bi, ti: (layer, 0, 0), pipeline_mode=pl.Buffered(1)),
            const2((1, MOBA_WIDTH)),
            const2((1, MOBA_WIDTH)),
            const2((depth, HGRN_WIDTH)),
            const2((MOBA_WIDTH, MOBA_WIDTH)),
        ],
        out_specs=[o[0] for o in outs],
        out_shape=[o[1] for o in outs],
        scratch_shapes=[pltpu.VMEM((MASK_LANES, MOBA_WIDTH), F32)],
        compiler_params=pltpu.CompilerParams(
            dimension_semantics=("arbitrary", "arbitrary"), vmem_limit_bytes=VMEM_LIMIT_BYTES),
        name=f"in_proj_l{layer}",
    )(rel_bias.astype(F32), h, norm_gain, w_in, q_gain, k_gain, lb_logits, hmean)


def _moba_kernel(qaug_ref, qnext_ref, k_ref, vt_ref, bias_ref, out_ref, s0_ref, s1_ref, max0_ref, max1_ref):
    blk = pl.program_id(2)
    tq = MOBA_BLOCK
    group_rows = MOBA_GROUP * MOBA_BLOCK

    def score_head(g, h, bufs, near, q_ref=qaug_ref, q_blk=blk):
        s_ref, max_ref = bufs
        words = pl.ds(pl.multiple_of(g * (group_rows // 2), group_rows // 2), group_rows // 2)
        k_rows = pltpu.bitcast(k_ref[0, h, words, :], BF16)
        s = _dot(k_rows, q_ref[0, h, 0])
        if near:
            parts = []
            for u in range(MOBA_GROUP):
                dist = q_blk - (g * MOBA_GROUP + u)
                tile = bias_ref[h, jnp.where((dist >= 0) & (dist < NEAR_BLOCKS), dist, NEAR_BLOCKS)]
                parts.append(s[u * MOBA_BLOCK:(u + 1) * MOBA_BLOCK] + tile)
            s = jnp.concatenate(parts, axis=0)
        s_ref[h] = s
        max_ref[h] = jnp.max(s.reshape(group_rows // SUBLANES, SUBLANES, tq), axis=0)

    def softmax_head(g, h, bufs, m, acc):
        s_ref, max_ref = bufs
        rows = pl.ds(pl.multiple_of(g * group_rows, group_rows), group_rows)
        m_new = jnp.maximum(m, jnp.max(max_ref[h], axis=0, keepdims=True))
        p = jnp.exp2(s_ref[h] - m_new).astype(BF16)
        acc = jnp.exp2(m - m_new) * acc + _dot(pltpu.bitcast(vt_ref[0, h, :, rows], BF16), p)
        return m_new, acc

    heads = range(MOBA_HEADS_PER_STEP)
    state = []
    for _ in heads:
        state += [jnp.full((1, tq), MASK_VALUE, F32), jnp.zeros((VT_ROWS, tq), F32)]
    bufs = ((s0_ref, max0_ref), (s1_ref, max1_ref))

    def group_pair(i, state, near_next):
        state = list(state)
        for parity in range(2):
            for h in heads:
                state[2 * h], state[2 * h + 1] = softmax_head(
                    2 * i + parity, h, bufs[parity], state[2 * h], state[2 * h + 1])
                score_head(2 * i + 2 + parity, h, bufs[parity], near_next)
        return state

    pair_blocks = 2 * MOBA_GROUP
    last = blk // pair_blocks
    n_far = jnp.maximum(jnp.maximum(blk - (NEAR_BLOCKS - 1), 0) // pair_blocks - 1, 0)

    @pl.when(blk == 0)
    def _():
        for parity in range(2):
            for h in heads:
                score_head(parity, h, bufs[parity], True)

    def pairs(lo, hi, state, near_next):
        def two(j, st):
            return tuple(group_pair(lo + 2 * j + 1, group_pair(lo + 2 * j, st, near_next), near_next))

        count = hi - lo
        state = lax.fori_loop(0, count // 2, two, tuple(state))
        return lax.cond(count % 2 == 1, lambda st: tuple(group_pair(hi - 1, st, near_next)),
                        lambda st: tuple(st), state)

    state = pairs(0, n_far, state, False)
    state = pairs(n_far, last, state, True)

    def last_pair(groups):
        def run(st):
            st = list(st)
            for parity in range(groups):
                for h in heads:
                    st[2 * h], st[2 * h + 1] = softmax_head(
                        2 * last + parity, h, bufs[parity], st[2 * h], st[2 * h + 1])
                    for nxt in range(parity, 2 if parity == groups - 1 else parity + 1):
                        score_head(nxt, h, bufs[nxt], True, qnext_ref, blk + 1)
            return tuple(st)
        return run

    second = (blk % pair_blocks) >= MOBA_GROUP
    state = lax.cond(second, last_pair(2), last_pair(1), tuple(state))

    accs = state[1::2]
    out_t = jnp.concatenate([a[:MOBA_HEAD_DIM] / a[MOBA_HEAD_DIM:MOBA_HEAD_DIM + 1] for a in accs], axis=0)
    out_ref[0] = out_t.T.astype(out_ref.dtype)


def _moba(qaug, k, vt, bias_tiles):
    b, _, _, s = vt.shape
    nb = s // MOBA_BLOCK
    assert nb % (2 * MOBA_GROUP) == 0
    hs = MOBA_HEADS_PER_STEP
    width = hs * MOBA_HEAD_DIM
    return pl.pallas_call(
        _moba_kernel,
        grid=(b, MOBA_HEADS // hs, nb),
        in_specs=[
            pl.BlockSpec((1, hs, 1, LANES, MOBA_BLOCK), lambda bi, pi, ti: (bi, pi, ti, 0, 0)),
            pl.BlockSpec((1, hs, 1, LANES, MOBA_BLOCK),
                         lambda bi, pi, ti: (bi, pi, jnp.minimum(ti + 1, nb - 1), 0, 0)),
            pl.BlockSpec((1, hs, s // 2, LANES), lambda bi, pi, ti: (bi, pi, 0, 0)),
            pl.BlockSpec((1, hs, VT_ROWS // 2, s), lambda bi, pi, ti: (bi, pi, 0, 0)),
            pl.BlockSpec((hs, NEAR_BLOCKS + 1, MOBA_BLOCK, MOBA_BLOCK),
                         lambda bi, pi, ti: (pi, 0, 0, 0), pipeline_mode=pl.Buffered(1)),
        ],
        out_specs=pl.BlockSpec((1, MOBA_BLOCK, width), lambda bi, pi, ti: (bi, ti, pi)),
        out_shape=jax.ShapeDtypeStruct((b, s, MOBA_WIDTH), BF16),
        scratch_shapes=[pltpu.VMEM((hs, MOBA_GROUP * MOBA_BLOCK, MOBA_BLOCK), F32),
                        pltpu.VMEM((hs, MOBA_GROUP * MOBA_BLOCK, MOBA_BLOCK), F32),
                        pltpu.VMEM((hs, SUBLANES, MOBA_BLOCK), F32),
                        pltpu.VMEM((hs, SUBLANES, MOBA_BLOCK), F32)],
        compiler_params=pltpu.CompilerParams(
            dimension_semantics=("arbitrary", "arbitrary", "arbitrary"), vmem_limit_bytes=VMEM_LIMIT_BYTES),
        name="moba_attention",
    )(qaug, qaug, k, vt, bias_tiles)


def _chunk_cumsum(g):
    groups = HGRN_CHUNK // SUBLANES
    g3 = g.reshape(groups, SUBLANES, g.shape[-1])
    sub = lax.broadcasted_iota(jnp.int32, g3.shape, 1)
    shift = 1
    while shift < SUBLANES:
        g3 = g3 + jnp.where(sub >= shift, pltpu.roll(g3, shift, axis=1), 0.0)
        shift *= 2
    rows = []
    run = None
    for i in range(groups):
        cur = g3[i] if run is None else g3[i] + run
        rows.append(cur)
        run = cur[SUBLANES - 1:SUBLANES, :]
    return jnp.concatenate(rows, axis=0)


def _hgrn_masks():
    c = HGRN_CHUNK
    row = lax.broadcasted_iota(jnp.int32, (c, 1), 0)
    rt = lax.broadcasted_iota(jnp.int32, (c, c), 0)
    cs = lax.broadcasted_iota(jnp.int32, (c, c), 1)
    levels = []
    for half in HGRN_LEVELS:
        shift = int(math.log2(2 * half))
        valid = ((rt >> shift) == (cs >> shift)) & ((rt & half) != 0) & ((cs & half) == 0)
        levels.append(((row & half) != 0, valid))
    return levels, rt == cs


def _hgrn_chunk_matmuls(q, k, v, g2, state_t, level_masks):
    c = HGRN_CHUNK
    cum = _chunk_cumsum(g2)
    last = cum[c - 1:c, :]
    v16 = v.astype(BF16)

    o = _dot_nt((q * jnp.exp2(cum)).astype(BF16), state_t.astype(BF16))
    k_end = (k * jnp.exp2(last - cum)).astype(BF16)
    new_state = state_t * jnp.exp2(last) + _dot_tn(v16, k_end)

    scores = jnp.zeros((c, c), F32)
    groups = c // SUBLANES
    width = cum.shape[-1]
    cum3 = cum.reshape(groups, SUBLANES, width)
    sub = lax.broadcasted_iota(jnp.int32, (groups, SUBLANES, width), 1)

    def row_of_group(r):
        return jnp.broadcast_to(cum3[:, r:r + 1, :], cum3.shape)

    for half, (is_q, valid) in zip(HGRN_LEVELS, level_masks):
        if half == 1:
            exponent = jnp.where(is_q, g2, 0.0)
        else:
            if half >= SUBLANES:
                ref = jnp.concatenate(
                    [jnp.broadcast_to(cum[lo + half - 1:lo + half, :], (2 * half, width))
                     for lo in range(0, c, 2 * half)], axis=0)
            elif half == 4:
                ref = row_of_group(3).reshape(c, width)
            else:
                ref = jnp.where(sub < 4, row_of_group(1), row_of_group(5)).reshape(c, width)
            exponent = jnp.where(is_q, cum - ref, ref - cum)
        x = (jnp.where(is_q, q, k) * jnp.exp2(exponent)).astype(BF16)
        scores = scores + jnp.where(valid, _dot_nt(x, x), 0.0)
    return o, new_state, scores, v16


def _hgrn_kernel(q_ref, k_ref, v_ref, g_ref, out_ref, state_ref):
    @pl.when(pl.program_id(1) == 0)
    def _():
        state_ref[...] = jnp.zeros_like(state_ref)

    n_chunks = q_ref.shape[1] // HGRN_CHUNK
    masks = _hgrn_masks()

    level_masks, on_diagonal = masks

    def chunk(ci, carry):
        rows = pl.ds(pl.multiple_of(ci * HGRN_CHUNK, HGRN_CHUNK), HGRN_CHUNK)
        partial = []
        for h in range(HGRN_HEADS):
            cols = slice(h * HGRN_DIM, (h + 1) * HGRN_DIM)
            q, k = q_ref[0, rows, cols], k_ref[0, rows, cols]
            o, new_state, scores, v16 = _hgrn_chunk_matmuls(
                q, k, v_ref[0, rows, cols], g_ref[0, rows, cols], state_ref[h], level_masks)
            state_ref[h] = new_state
            partial.append((jnp.sum(q * k, axis=-1, keepdims=True), o, scores, v16))
        for h, (self_score, o, scores, v16) in enumerate(partial):
            scores = jnp.where(on_diagonal, self_score, scores)
            out = o + _dot(scores.astype(BF16), v16)
            out_ref[0, rows, h * HGRN_DIM:(h + 1) * HGRN_DIM] = out.astype(out_ref.dtype)
        return carry

    lax.fori_loop(0, n_chunks, chunk, 0, unroll=n_chunks)


def _hgrn(hq, hk, hv, hg):
    b, s, w = hq.shape
    ts = min(s, 1024)
    spec = pl.BlockSpec((1, ts, w), lambda bi, ti: (bi, ti, 0))
    return pl.pallas_call(
        _hgrn_kernel,
        grid=(b, s // ts),
        in_specs=[spec, spec, spec, spec],
        out_specs=spec,
        out_shape=jax.ShapeDtypeStruct((b, s, w), BF16),
        scratch_shapes=[pltpu.VMEM((HGRN_HEADS, HGRN_DIM, HGRN_DIM), F32)],
        compiler_params=pltpu.CompilerParams(
            dimension_semantics=("arbitrary", "arbitrary"), vmem_limit_bytes=VMEM_LIMIT_BYTES),
        name="hgrn2_recurrence",
    )(hq, hk, hv, hg)


def _merge_kernel(h_ref, ya_ref, ag_ref, yb_ref, bg_ref, ga_ref, gb_ref, p_ref, og_ref,
                  wa_ref, wb_ref, wo_ref, wp_ref, wg_ref, out_ref):
    ya = _dot(ya_ref[0] * ag_ref[0], wa_ref[0])

    yb = yb_ref[0].astype(F32)
    normed = []
    for h in range(HGRN_HEADS):
        t = yb[:, h * HGRN_DIM:(h + 1) * HGRN_DIM]
        normed.append(t * lax.rsqrt(jnp.mean(t * t, axis=-1, keepdims=True) + RMS_EPS))
    yb = jnp.concatenate(normed, axis=-1) * og_ref[...]
    yb = _dot((yb * bg_ref[0]).astype(BF16), wb_ref[0])

    merged = ga_ref[0] * ya + gb_ref[0] * yb
    h = h_ref[0] + _dot(merged.astype(BF16), wo_ref[0])
    ple = _dot(p_ref[0, 0].astype(BF16), wp_ref[0]) * _sigmoid(_dot(h.astype(BF16), wg_ref[0]))
    out_ref[0] = h + ple


def _merge(layer, h, ya, ag, yb, bg, ga, gb, p, out_gain, w_up_a, w_up_b, w_out, w_ple, w_ple_gate):
    b, s, d = h.shape
    tm = min(s, 1024)
    tok = lambda width: pl.BlockSpec((1, tm, width), lambda bi, ti: (bi, ti, 0))
    wspec = lambda w: pl.BlockSpec((1,) + w.shape[1:], lambda bi, ti: (layer, 0, 0), pipeline_mode=pl.Buffered(1))
    return pl.pallas_call(
        _merge_kernel,
        grid=(b, s // tm),
        in_specs=[
            tok(d), tok(MOBA_WIDTH), tok(MOBA_WIDTH), tok(HGRN_WIDTH), tok(HGRN_WIDTH), tok(d), tok(d),
            pl.BlockSpec((1, 1, tm, p.shape[-1]), lambda bi, ti: (layer, bi, ti, 0)),
            pl.BlockSpec((1, HGRN_WIDTH), lambda bi, ti: (0, 0)),
            wspec(w_up_a), wspec(w_up_b), wspec(w_out), wspec(w_ple), wspec(w_ple_gate),
        ],
        out_specs=tok(d),
        out_shape=jax.ShapeDtypeStruct((b, s, d), F32),
        compiler_params=pltpu.CompilerParams(
            dimension_semantics=("arbitrary", "arbitrary"), vmem_limit_bytes=VMEM_LIMIT_BYTES),
        name=f"merge_l{layer}",
    )(h, ya, ag, yb, bg, ga, gb, p, out_gain, w_up_a, w_up_b, w_out, w_ple, w_ple_gate)


def kernel(x, p, norm_gain, w_in, q_norm_gain, k_norm_gain, rel_bias, hgrn_lb_logits, hgrn_out_gain,
           w_up_a, w_up_b, w_out, w_ple, w_ple_gate):
    b, s, d = x.shape
    depth = w_in.shape[0]
    assert s % MOBA_BLOCK == 0 and s % HGRN_CHUNK == 0

    w_in, w_up_a, w_up_b, w_out, w_ple, w_ple_gate = (
        w.astype(BF16) for w in (w_in, w_up_a, w_up_b, w_out, w_ple, w_ple_gate))
    head_id = jnp.arange(MOBA_WIDTH) // MOBA_HEAD_DIM
    hmean = ((head_id[:, None] == head_id[None, :]).astype(F32) / MOBA_HEAD_DIM).astype(BF16)
    tile = lambda g: jnp.tile(g.astype(F32), MOBA_HEADS)[None, :]

    bias_tiles = _bias_tiles(rel_bias)
    h = x
    for i in range(depth):
        qaug, k, vt, ag, hq, hk, hv, hg, bg, ga, gb = _in_proj(
            i, rel_bias, h, norm_gain[i][None, :].astype(F32), w_in, tile(q_norm_gain[i]), tile(k_norm_gain[i]),
            hgrn_lb_logits.astype(F32), hmean)
        ya = _moba(qaug, k, vt, bias_tiles)
        yb = _hgrn(hq, hk, hv, hg)
        h = _merge(i, h, ya, ag, yb, bg, ga, gb, p, hgrn_out_gain[i][None, :].astype(F32),
                   w_up_a, w_up_b, w_out, w_ple, w_ple_gate)
    return h
```
